```python
import jax, jax.numpy as jnp
from jax import lax
import numpy as np

D_MODEL = 4096
BATCH = 4
SEQ = 4096
DEPTH = 1

HEAD_DIM = 128
N_HEADS = (D_MODEL // 2) // HEAD_DIM
N_KV_HEADS = N_HEADS // 4
GQA_GROUP = N_HEADS // N_KV_HEADS
WINDOW = 128
BLOCK = 128
CONV_WIDTH = D_MODEL // 2
CONV_K = 3
Q_W = N_HEADS * HEAD_DIM
KV_W = N_KV_HEADS * HEAD_DIM
GATE_W = D_MODEL
OFF_Q = 0
OFF_K = OFF_Q + Q_W
OFF_V = OFF_K + KV_W
OFF_H = OFF_V + KV_W
OFF_B = OFF_H + CONV_WIDTH
OFF_C = OFF_B + CONV_WIDTH
OFF_GA = OFF_C + CONV_WIDTH
OFF_GC = OFF_GA + GATE_W
IN_W = OFF_GC + GATE_W
PEER_HEADS = 8
N_KEYS = 128
N_EXPERTS = N_KEYS * N_KEYS
PEER_KEY_DIM = 128
PEER_TOPK = 16
PEER_CHUNK = 128
EPS = 1e-6

kernel_name = "hybrid_swa_shortconv_peer_encoder"


def rms_norm(x, g):
    xf = x.astype(jnp.float32)
    y = xf * lax.rsqrt(jnp.mean(xf * xf, axis=-1, keepdims=True) + EPS)
    return (y * g.astype(jnp.float32)).astype(x.dtype)


def alibi_slopes():
    h = jnp.arange(1, N_HEADS + 1, dtype=jnp.float32)
    return jnp.exp2(-8.0 * h / N_HEADS).reshape(N_KV_HEADS, GQA_GROUP)


def banded_window_attention(q, k, v, sink_logits):
    b, s = q.shape[0], q.shape[1]
    nb = s // BLOCK
    qb = q.reshape(b, nb, BLOCK, N_KV_HEADS, GQA_GROUP, HEAD_DIM)

    def neighbour_blocks(t):
        tp = jnp.pad(t, ((0, 0), (BLOCK, BLOCK), (0, 0), (0, 0)))
        tb = tp.reshape(b, nb + 2, BLOCK, N_KV_HEADS, HEAD_DIM)
        return jnp.concatenate([tb[:, :-2], tb[:, 1:-1], tb[:, 2:]], axis=2)

    kb = neighbour_blocks(k)
    vb = neighbour_blocks(v)
    scores = jnp.einsum('bnqhgd,bnshd->bnhgqs', qb, kb,
                        preferred_element_type=jnp.float32) * (HEAD_DIM ** -0.5)
    qi = jnp.arange(BLOCK)[:, None]
    kj = jnp.arange(3 * BLOCK)[None, :]
    dist = jnp.abs(kj - BLOCK - qi)
    key_pos = (jnp.arange(nb)[:, None] - 1) * BLOCK + jnp.arange(3 * BLOCK)[None, :]
    in_range = (key_pos >= 0) & (key_pos < s)
    mask = (dist <= WINDOW)[None, :, :] & in_range[:, None, :]
    slopes = alibi_slopes()
    scores = scores - slopes[:, :, None, None] * dist.astype(jnp.float32)
    scores = jnp.where(mask[None, :, None, None], scores, -jnp.inf)
    sink = sink_logits.astype(jnp.float32).reshape(N_KV_HEADS, GQA_GROUP)[:, :, None, None]
    m = jnp.maximum(jnp.max(scores, axis=-1, keepdims=True), sink)
    p = jnp.exp(scores - m)
    denom = jnp.sum(p, axis=-1, keepdims=True) + jnp.exp(sink - m)
    probs = (p / denom).astype(v.dtype)
    out = jnp.einsum('bnhgqs,bnshd->bnqhgd', probs, vb)
    return out.reshape(b, s, Q_W)


def short_gated_conv(h, gate_b, gate_c, conv_w):
    u = gate_c * h
    half = CONV_K // 2
    s = u.shape[1]
    up = jnp.pad(u, ((0, 0), (half, half), (0, 0)))
    y = sum(conv_w[j] * up[:, j:j + s] for j in range(CONV_K))
    return gate_b * y


def peer_ffn(x, w_q_peer, sub_keys, w_down, w_up):
    b, s, d = x.shape
    t = b * s
    xt = x.reshape(t, d)
    q = (xt @ w_q_peer).reshape(t, PEER_HEADS, 2, PEER_KEY_DIM)
    s1 = jnp.einsum('thk,hnk->thn', q[:, :, 0], sub_keys[:, 0], preferred_element_type=jnp.float32)
    s2 = jnp.einsum('thk,hnk->thn', q[:, :, 1], sub_keys[:, 1], preferred_element_type=jnp.float32)
    v1, i1 = lax.top_k(s1, PEER_TOPK)
    v2, i2 = lax.top_k(s2, PEER_TOPK)
    cand = (v1[..., :, None] + v2[..., None, :]).reshape(t, PEER_HEADS, PEER_TOPK * PEER_TOPK)
    cand_idx = (i1[..., :, None] * N_KEYS + i2[..., None, :]).reshape(t, PEER_HEADS, PEER_TOPK * PEER_TOPK)
    top_s, pos = lax.top_k(cand, PEER_TOPK)
    idx = jnp.take_along_axis(cand_idx, pos, axis=-1)
    gates = jax.nn.softmax(top_s, axis=-1)
    n_chunks = t // PEER_CHUNK

    def chunk(args):
        xc, ic, gc = args
        u = jnp.take(w_down, ic, axis=0)
        a = jax.nn.gelu(jnp.einsum('cd,chkd->chk', xc, u, preferred_element_type=jnp.float32),
                        approximate=False)
        wgt = (gc * a).astype(xc.dtype)
        vv = jnp.take(w_up, ic, axis=0)
        return jnp.einsum('chk,chkd->cd', wgt, vv)

    out = lax.map(chunk, (xt.reshape(n_chunks, PEER_CHUNK, d),
                          idx.reshape(n_chunks, PEER_CHUNK, PEER_HEADS, PEER_TOPK),
                          gates.reshape(n_chunks, PEER_CHUNK, PEER_HEADS, PEER_TOPK)))
    return out.reshape(b, s, d)


def setup_inputs(seed: int = 0) -> dict:
    key = jax.random.key(seed)
    ks = jax.random.split(key, 16)
    f32 = jnp.float32
    nrm = lambda k, shape, scale: jax.random.normal(k, shape, f32) * scale
    L = DEPTH
    return {
        "x": nrm(ks[0], (BATCH, SEQ, D_MODEL), 1.0),
        "norm1_g": 1.0 + nrm(ks[1], (L, D_MODEL), 0.01),
        "w_in": nrm(ks[2], (L, D_MODEL, IN_W), D_MODEL ** -0.5),
        "q_norm_g": 1.0 + nrm(ks[3], (L, HEAD_DIM), 0.01),
        "k_norm_g": 1.0 + nrm(ks[4], (L, HEAD_DIM), 0.01),
        "sink_logits": nrm(ks[5], (L, N_HEADS), 0.5),
        "conv_w": nrm(ks[6], (L, CONV_K, CONV_WIDTH), CONV_K ** -0.5),
        "w_o_attn": nrm(ks[7], (L, Q_W, D_MODEL), Q_W ** -0.5),
        "w_o_conv": nrm(ks[8], (L, CONV_WIDTH, D_MODEL), CONV_WIDTH ** -0.5),
        "w_out": nrm(ks[9], (L, D_MODEL, D_MODEL), D_MODEL ** -0.5),
        "norm2_g": 1.0 + nrm(ks[10], (L, D_MODEL), 0.01),
        "w_q_peer": nrm(ks[11], (L, D_MODEL, PEER_HEADS * 2 * PEER_KEY_DIM), D_MODEL ** -0.5),
        "sub_keys": nrm(ks[12], (L, PEER_HEADS, 2, N_KEYS, PEER_KEY_DIM), PEER_KEY_DIM ** -0.5),
        "w_down": nrm(ks[13], (L, N_EXPERTS, D_MODEL), D_MODEL ** -0.5),
        "w_up": nrm(ks[14], (L, N_EXPERTS, D_MODEL), PEER_HEADS ** -0.5 * PEER_TOPK ** -0.5),
    }


def reference(x, norm1_g, w_in, q_norm_g, k_norm_g, sink_logits, conv_w, w_o_attn, w_o_conv,
              w_out, norm2_g, w_q_peer, sub_keys, w_down, w_up):
    b, s, _ = x.shape
    for i in range(DEPTH):
        xn = rms_norm(x, norm1_g[i])
        proj = xn @ w_in[i]
        q = rms_norm(proj[..., OFF_Q:OFF_K].reshape(b, s, N_HEADS, HEAD_DIM), q_norm_g[i])
        k = rms_norm(proj[..., OFF_K:OFF_V].reshape(b, s, N_KV_HEADS, HEAD_DIM), k_norm_g[i])
        v = proj[..., OFF_V:OFF_H].reshape(b, s, N_KV_HEADS, HEAD_DIM)
        attn = banded_window_attention(q, k, v, sink_logits[i])
        conv = short_gated_conv(proj[..., OFF_H:OFF_B], proj[..., OFF_B:OFF_C],
                                proj[..., OFF_C:OFF_GA], conv_w[i])
        branch_a = attn @ w_o_attn[i]
        branch_c = conv @ w_o_conv[i]
        merged = (jax.nn.sigmoid(proj[..., OFF_GA:OFF_GC]) * branch_a
                  + jax.nn.sigmoid(proj[..., OFF_GC:IN_W]) * branch_c)
        x = x + merged @ w_out[i]
        x = x + peer_ffn(rms_norm(x, norm2_g[i]), w_q_peer[i], sub_keys[i], w_down[i], w_up[i])
    return x
```

```python
import functools

import jax
import jax.numpy as jnp
from jax import lax
from jax.experimental import pallas as pl
from jax.experimental.pallas import tpu as pltpu

F32 = jnp.float32
BF16 = jnp.bfloat16

HEAD_DIM = 128
GQA_GROUP = 4
WINDOW = 128
BLOCK = 128
CONV_K = 3
PEER_HEADS = 8
N_KEYS = 128
PEER_KEY_DIM = 128
PEER_TOPK = 16
EPS = 1e-6
SUBLANES = 8
VMEM_LIMIT = 56 * 1024 * 1024


def _params(sem):
    return pltpu.CompilerParams(dimension_semantics=sem, vmem_limit_bytes=VMEM_LIMIT)


def _rmsnorm_kernel(x_ref, g_ref, o_ref, *, transpose):
    x = x_ref[...]
    ms = jnp.mean(x * x, axis=-1, keepdims=True)
    y = x * lax.rsqrt(ms + EPS) * g_ref[...]
    if transpose:
        y = y.T
    o_ref[...] = y.astype(o_ref.dtype)


def _rmsnorm(x, g, *, tm, transpose=False):
    t, d = x.shape
    if transpose:
        out_shape = jax.ShapeDtypeStruct((d, t), BF16)
        out_spec = pl.BlockSpec((d, tm), lambda i: (0, i))
    else:
        out_shape = jax.ShapeDtypeStruct((t, d), BF16)
        out_spec = pl.BlockSpec((tm, d), lambda i: (i, 0))
    return pl.pallas_call(
        functools.partial(_rmsnorm_kernel, transpose=transpose),
        grid=(t // tm,),
        in_specs=[pl.BlockSpec((tm, d), lambda i: (i, 0)),
                  pl.BlockSpec((1, d), lambda i: (0, 0))],
        out_specs=out_spec,
        out_shape=out_shape,
        compiler_params=_params(("parallel",)),
        name="rmsnorm_t" if transpose else "rmsnorm",
    )(x, g.reshape(1, d))


def _mm_kernel(a_ref, b_ref, o_ref):
    o_ref[...] = jnp.dot(a_ref[...], b_ref[...],
                         preferred_element_type=F32).astype(o_ref.dtype)


def _matmul(a, b, *, tm, tn, out_dtype, name):
    m, k = a.shape
    n = b.shape[1]
    return pl.pallas_call(
        _mm_kernel,
        grid=(m // tm, n // tn),
        in_specs=[pl.BlockSpec((tm, k), lambda i, j: (i, 0)),
                  pl.BlockSpec((k, tn), lambda i, j: (0, j))],
        out_specs=pl.BlockSpec((tm, tn), lambda i, j: (i, j)),
        out_shape=jax.ShapeDtypeStruct((m, n), out_dtype),
        compiler_params=_params(("parallel", "arbitrary")),
        name=name,
    )(a, b)


def _mm_res_kernel(r_ref, a_ref, b_ref, o_ref):
    o_ref[...] = r_ref[...] + jnp.dot(a_ref[...], b_ref[...], preferred_element_type=F32)


def _matmul_residual(r, a, b, *, tm, tn):
    m, k = a.shape
    n = b.shape[1]
    return pl.pallas_call(
        _mm_res_kernel,
        grid=(m // tm, n // tn),
        in_specs=[pl.BlockSpec((tm, tn), lambda i, j: (i, j)),
                  pl.BlockSpec((tm, k), lambda i, j: (i, 0)),
                  pl.BlockSpec((k, tn), lambda i, j: (0, j))],
        out_specs=pl.BlockSpec((tm, tn), lambda i, j: (i, j)),
        out_shape=jax.ShapeDtypeStruct((m, n), F32),
        compiler_params=_params(("parallel", "arbitrary")),
        name="out_proj_residual",
    )(r, a, b)


def _merge_kernel(attn_ref, conv_ref, woa_ref, woc_ref, ga_ref, gc_ref, o_ref):
    a = jnp.dot(attn_ref[...], woa_ref[...], preferred_element_type=F32)
    c = jnp.dot(conv_ref[...], woc_ref[...], preferred_element_type=F32)
    ga = jax.nn.sigmoid(ga_ref[...].astype(F32))
    gc = jax.nn.sigmoid(gc_ref[...].astype(F32))
    o_ref[...] = (ga * a + gc * c).astype(o_ref.dtype)


def _merge(attn, conv, woa, woc, proj, *, off_ga, off_gc, tm, tn):
    t, qw = attn.shape
    cw = conv.shape[1]
    d = woa.shape[1]
    ga_blk, gc_blk = off_ga // tn, off_gc // tn
    return pl.pallas_call(
        _merge_kernel,
        grid=(t // tm, d // tn),
        in_specs=[pl.BlockSpec((tm, qw), lambda i, j: (i, 0)),
                  pl.BlockSpec((tm, cw), lambda i, j: (i, 0)),
                  pl.BlockSpec((qw, tn), lambda i, j: (0, j)),
                  pl.BlockSpec((cw, tn), lambda i, j: (0, j)),
                  pl.BlockSpec((tm, tn), lambda i, j: (i, ga_blk + j)),
                  pl.BlockSpec((tm, tn), lambda i, j: (i, gc_blk + j))],
        out_specs=pl.BlockSpec((tm, tn), lambda i, j: (i, j)),
        out_shape=jax.ShapeDtypeStruct((t, d), BF16),
        compiler_params=_params(("parallel", "arbitrary")),
        name="merge_proj",
    )(attn, conv, woa, woc, proj, proj)


def _head_rmsnorm(x, g):
    ms = jnp.mean(x * x, axis=-1, keepdims=True)
    return x * lax.rsqrt(ms + EPS) * g


def _attn_kernel(sink_ref, q_ref, kp_ref, kc_ref, kn_ref, vp_ref, vc_ref, vn_ref,
                 qg_ref, kg_ref, o_ref, *, nb, n_kv, n_heads):
    n = pl.program_id(0) % nb
    has_prev = n > 0
    has_next = n < nb - 1
    qi = lax.broadcasted_iota(jnp.int32, (BLOCK, 3 * BLOCK), 0)
    kj = lax.broadcasted_iota(jnp.int32, (BLOCK, 3 * BLOCK), 1)
    dist = jnp.abs(kj - BLOCK - qi)
    valid = ((dist <= WINDOW)
             & ((kj >= BLOCK) | has_prev)
             & ((kj < 2 * BLOCK) | has_next))
    distf = dist.astype(F32)
    qg = qg_ref[...]
    kg = kg_ref[...]
    scale = HEAD_DIM ** -0.5
    for kv in range(n_kv):
        cs = slice(kv * HEAD_DIM, (kv + 1) * HEAD_DIM)
        k = jnp.concatenate([kp_ref[:, cs], kc_ref[:, cs], kn_ref[:, cs]], axis=0)
        kb = _head_rmsnorm(k.astype(F32), kg).astype(BF16)
        v = jnp.concatenate([vp_ref[:, cs], vc_ref[:, cs], vn_ref[:, cs]], axis=0)
        for g in range(GQA_GROUP):
            hq = kv * GQA_GROUP + g
            hs = slice(hq * HEAD_DIM, (hq + 1) * HEAD_DIM)
            slope = 2.0 ** (-8.0 * (hq + 1) / n_heads)
            qb = _head_rmsnorm(q_ref[:, hs].astype(F32), qg).astype(BF16)
            s = lax.dot_general(qb, kb, (((1,), (1,)), ((), ())),
                                preferred_element_type=F32)
            s = s * scale - slope * distf
            s = jnp.where(valid, s, -jnp.inf)
            sink = sink_ref[hq]
            m = jnp.maximum(jnp.max(s, axis=-1, keepdims=True), sink)
            p = jnp.exp(s - m)
            denom = jnp.sum(p, axis=-1, keepdims=True) + jnp.exp(sink - m)
            probs = (p / denom).astype(BF16)
            o = jnp.dot(probs, v, preferred_element_type=F32)
            o_ref[:, hs] = o.astype(o_ref.dtype)


def _attention(proj, sink, qg, kg, *, seq, q_w, kv_w):
    t = proj.shape[0]
    nb = seq // BLOCK
    n_heads = q_w // HEAD_DIM
    n_kv = kv_w // HEAD_DIM
    k_blk = q_w // kv_w
    v_blk = k_blk + 1

    def prev_row(r):
        return jnp.where(r % nb > 0, r - 1, r)

    def next_row(r):
        return jnp.where(r % nb < nb - 1, r + 1, r)

    kv_specs = [pl.BlockSpec((BLOCK, kv_w), lambda r, c=c, f=f: (f(r), c))
                for c in (k_blk, v_blk) for f in (prev_row, lambda r: r, next_row)]
    return pl.pallas_call(
        functools.partial(_attn_kernel, nb=nb, n_kv=n_kv, n_heads=n_heads),
        grid=(t // BLOCK,),
        in_specs=[pl.BlockSpec(memory_space=pltpu.SMEM),
                  pl.BlockSpec((BLOCK, q_w), lambda r: (r, 0))]
                 + kv_specs
                 + [pl.BlockSpec((1, HEAD_DIM), lambda r: (0, 0)),
                    pl.BlockSpec((1, HEAD_DIM), lambda r: (0, 0))],
        out_specs=pl.BlockSpec((BLOCK, q_w), lambda r: (r, 0)),
        out_shape=jax.ShapeDtypeStruct((t, q_w), BF16),
        compiler_params=_params(("parallel",)),
        name="banded_attention",
    )(sink, proj, proj, proj, proj, proj, proj, proj,
      qg.reshape(1, HEAD_DIM), kg.reshape(1, HEAD_DIM))


def _conv_kernel(h_ref, b_ref, c_ref, hp_ref, cp_ref, hn_ref, cn_ref, w_ref, o_ref,
                 *, tiles_per_seq):
    i = pl.program_id(0)
    tm = h_ref.shape[0]
    u = c_ref[...].astype(F32) * h_ref[...].astype(F32)
    first = (i % tiles_per_seq) == 0
    last = (i % tiles_per_seq) == tiles_per_seq - 1
    u_prev = (cp_ref[SUBLANES - 1:SUBLANES, :].astype(F32)
              * hp_ref[SUBLANES - 1:SUBLANES, :].astype(F32))
    u_next = cn_ref[0:1, :].astype(F32) * hn_ref[0:1, :].astype(F32)
    u_prev = jnp.where(first, 0.0, u_prev)
    u_next = jnp.where(last, 0.0, u_next)
    row = lax.broadcasted_iota(jnp.int32, u.shape, 0)
    u_m1 = jnp.where(row == 0, u_prev, pltpu.roll(u, 1, axis=0))
    u_p1 = jnp.where(row == tm - 1, u_next, pltpu.roll(u, tm - 1, axis=0))
    w = w_ref[...]
    y = w[0:1, :] * u_m1 + w[1:2, :] * u + w[2:3, :] * u_p1
    o_ref[...] = (b_ref[...].astype(F32) * y).astype(o_ref.dtype)


def _short_conv(proj, conv_w, *, seq, off_h, conv_width, tm):
    t = proj.shape[0]
    cw = conv_width // 2
    nc = conv_width // cw
    hb, bb, cb = off_h // cw, (off_h + conv_width) // cw, (off_h + 2 * conv_width) // cw
    rows8 = tm // SUBLANES
    last8 = t // SUBLANES - 1

    def prev8(i):
        return jnp.maximum(i * rows8 - 1, 0)

    def next8(i):
        return jnp.minimum((i + 1) * rows8, last8)

    w_pad = jnp.zeros((SUBLANES, conv_width), F32).at[:CONV_K].set(conv_w)
    return pl.pallas_call(
        functools.partial(_conv_kernel, tiles_per_seq=seq // tm),
        grid=(t // tm, nc),
        in_specs=[pl.BlockSpec((tm, cw), lambda i, j: (i, hb + j)),
                  pl.BlockSpec((tm, cw), lambda i, j: (i, bb + j)),
                  pl.BlockSpec((tm, cw), lambda i, j: (i, cb + j)),
                  pl.BlockSpec((SUBLANES, cw), lambda i, j: (prev8(i), hb + j)),
                  pl.BlockSpec((SUBLANES, cw), lambda i, j: (prev8(i), cb + j)),
                  pl.BlockSpec((SUBLANES, cw), lambda i, j: (next8(i), hb + j)),
                  pl.BlockSpec((SUBLANES, cw), lambda i, j: (next8(i), cb + j)),
                  pl.BlockSpec((SUBLANES, cw), lambda i, j: (0, j))],
        out_specs=pl.BlockSpec((tm, cw), lambda i, j: (i, j)),
        out_shape=jax.ShapeDtypeStruct((t, conv_width), BF16),
        compiler_params=_params(("parallel", "arbitrary")),
        name="short_conv",
    )(proj, proj, proj, proj, proj, proj, proj, w_pad)


def _extract_top(work, count):
    n = work.shape[0]
    iota = lax.broadcasted_iota(jnp.int32, work.shape, 0)
    vals = []
    for r in range(count):
        m = jnp.max(work, axis=0, keepdims=True)
        vals.append(m)
        if r < count - 1:
            first = jnp.min(jnp.where(work == m, iota, n), axis=0, keepdims=True)
            work = jnp.where(iota == first, -jnp.inf, work)
    return vals


def _peer_score_kernel(xt_ref, wq_ref, keys_ref, s2_ref, e2_ref, thr_ref, c_ref):
    qt = jnp.dot(wq_ref[...], xt_ref[...], preferred_element_type=F32)
    q1 = qt[:PEER_KEY_DIM].astype(BF16)
    q2 = qt[PEER_KEY_DIM:].astype(BF16)
    s1 = jnp.dot(keys_ref[0, 0], q1, preferred_element_type=F32)
    s2 = jnp.dot(keys_ref[0, 1], q2, preferred_element_type=F32)
    v1 = _extract_top(s1, PEER_TOPK)
    v2 = _extract_top(s2, PEER_TOPK)
    v2_all = jnp.concatenate(v2, axis=0)
    cand = jnp.concatenate([v1[a] + v2_all for a in range(PEER_TOPK)], axis=0)
    top = _extract_top(cand, PEER_TOPK + 1)
    m1, m2 = v1[0], v2[0]
    denom = jnp.exp(top[0] - (m1 + m2))
    for r in range(1, PEER_TOPK):
        denom = denom + jnp.exp(top[r] - (m1 + m2))
    tau = 0.5 * (top[PEER_TOPK - 1] + top[PEER_TOPK])
    tm = s1.shape[1]
    s2_ref[0] = s2
    e2_ref[0] = jnp.exp(s2 - m2)
    thr_ref[:, 0] = (tau - s1).reshape(N_KEYS // SUBLANES, SUBLANES, tm)
    c_ref[:, 0] = (jnp.exp(s1 - m1) / denom).reshape(N_KEYS // SUBLANES, SUBLANES, tm)


def _peer_score(xt, wq_t, keys, *, tm):
    d, t = xt.shape
    qd = 2 * PEER_KEY_DIM
    groups = N_KEYS // SUBLANES
    dense = jax.ShapeDtypeStruct((PEER_HEADS, N_KEYS, t), F32)
    grouped = jax.ShapeDtypeStruct((groups, PEER_HEADS, SUBLANES, t), F32)
    dense_spec = pl.BlockSpec((1, N_KEYS, tm), lambda i, h: (h, 0, i))
    grouped_spec = pl.BlockSpec((groups, 1, SUBLANES, tm), lambda i, h: (0, h, 0, i))
    return pl.pallas_call(
        _peer_score_kernel,
        grid=(t // tm, PEER_HEADS),
        in_specs=[pl.BlockSpec((d, tm), lambda i, h: (0, i)),
                  pl.BlockSpec((qd, d), lambda i, h: (h, 0)),
                  pl.BlockSpec((1, 2, N_KEYS, PEER_KEY_DIM), lambda i, h: (h, 0, 0, 0))],
        out_specs=[dense_spec, dense_spec, grouped_spec, grouped_spec],
        out_shape=[dense, dense, grouped, grouped],
        compiler_params=_params(("parallel", "arbitrary")),
        name="peer_score",
    )(xt, wq_t, keys)


def _gelu(x):
    return 0.5 * x * (1.0 + lax.erf(x * (2.0 ** -0.5)))


def _peer_mix_kernel(xt_ref, wd_ref, wu_ref, s2_ref, e2_ref, thr_ref, c_ref, o_ref, *, n_i):
    e = pl.program_id(1)

    @pl.when(e == 0)
    def _():
        o_ref[...] = jnp.zeros_like(o_ref)

    a = jnp.dot(wd_ref[...], xt_ref[...], preferred_element_type=F32)
    act = _gelu(a)
    groups_per_tile = SUBLANES // n_i
    sub = e % groups_per_tile
    pieces = []
    for ii in range(n_i):
        w = None
        for h in range(PEER_HEADS):
            thr_row = thr_ref[0, h, ii:ii + 1, :]
            c_row = c_ref[0, h, ii:ii + 1, :]
            for alt in range(1, groups_per_tile):
                r = alt * n_i + ii
                thr_row = jnp.where(sub == alt, thr_ref[0, h, r:r + 1, :], thr_row)
                c_row = jnp.where(sub == alt, c_ref[0, h, r:r + 1, :], c_row)
            term = jnp.where(s2_ref[h] >= thr_row, e2_ref[h] * c_row, 0.0)
            w = term if w is None else w + term
        pieces.append((act[ii * N_KEYS:(ii + 1) * N_KEYS] * w).astype(BF16))
    gated = jnp.concatenate(pieces, axis=0)
    o_ref[...] += jnp.dot(wu_ref[...], gated, preferred_element_type=F32)


def _peer_mix(xt, w_down, w_up_t, s2, e2, thr, c, *, tm, eb):
    d, t = xt.shape
    ne = w_down.shape[0]
    n_i = eb // N_KEYS
    groups_per_tile = SUBLANES // n_i
    dense_spec = pl.BlockSpec((PEER_HEADS, N_KEYS, tm), lambda i, e: (0, 0, i))
    grouped_spec = pl.BlockSpec((1, PEER_HEADS, SUBLANES, tm),
                                lambda i, e: (e // groups_per_tile, 0, 0, i))
    return pl.pallas_call(
        functools.partial(_peer_mix_kernel, n_i=n_i),
        grid=(t // tm, ne // eb),
        in_specs=[pl.BlockSpec((d, tm), lambda i, e: (0, i)),
                  pl.BlockSpec((eb, d), lambda i, e: (e, 0)),
                  pl.BlockSpec((d, eb), lambda i, e: (0, e)),
                  dense_spec, dense_spec, grouped_spec, grouped_spec],
        out_specs=pl.BlockSpec((d, tm), lambda i, e: (0, i)),
        out_shape=jax.ShapeDtypeStruct((d, t), F32),
        compiler_params=_params(("parallel", "arbitrary")),
        name="peer_mix",
    )(xt, w_down, w_up_t, s2, e2, thr, c)


def _add_t_kernel(x_ref, pt_ref, o_ref):
    o_ref[...] = x_ref[...] + pt_ref[...].T


def _add_transposed(x, pt, *, tm):
    t, d = x.shape
    return pl.pallas_call(
        _add_t_kernel,
        grid=(t // tm,),
        in_specs=[pl.BlockSpec((tm, d), lambda i: (i, 0)),
                  pl.BlockSpec((d, tm), lambda i: (0, i))],
        out_specs=pl.BlockSpec((tm, d), lambda i: (i, 0)),
        out_shape=jax.ShapeDtypeStruct((t, d), F32),
        compiler_params=_params(("parallel",)),
        name="add_transposed",
    )(x, pt)


def _tiles(t, d, in_w):
    def pick(n, cap):
        b = cap
        while n % b:
            b //= 2
        return b
    return dict(
        norm_tm=pick(t, 256),
        mm_tm=pick(t, 1024),
        mm_tn=pick(d // 4, 1024),
        conv_tm=pick(t, 512),
        peer_tm=pick(t, 512),
        peer_eb=512,
    )


def _layer(x2, seq, norm1_g, w_in, q_norm_g, k_norm_g, sink_logits, conv_w, w_o_attn,
           w_o_conv, w_out, norm2_g, w_q_peer, sub_keys, w_down, w_up):
    t, d = x2.shape
    in_w = w_in.shape[1]
    q_w = w_o_attn.shape[0]
    kv_w = q_w // GQA_GROUP
    conv_width = w_o_conv.shape[0]
    off_h = q_w + 2 * kv_w
    off_ga = off_h + 3 * conv_width
    off_gc = off_ga + d
    tl = _tiles(t, d, in_w)

    xn = _rmsnorm(x2, norm1_g, tm=tl["norm_tm"])
    proj = _matmul(xn, w_in.astype(BF16), tm=tl["mm_tm"], tn=tl["mm_tn"],
                   out_dtype=BF16, name="in_proj")
    attn = _attention(proj, sink_logits.astype(F32), q_norm_g, k_norm_g,
                      seq=seq, q_w=q_w, kv_w=kv_w)
    conv = _short_conv(proj, conv_w, seq=seq, off_h=off_h, conv_width=conv_width,
                       tm=tl["conv_tm"])
    merged = _merge(attn, conv, w_o_attn.astype(BF16), w_o_conv.astype(BF16), proj,
                    off_ga=off_ga, off_gc=off_gc, tm=tl["mm_tm"], tn=tl["mm_tn"])
    x1 = _matmul_residual(x2, merged, w_out.astype(BF16), tm=tl["mm_tm"], tn=tl["mm_tn"])

    xn2_t = _rmsnorm(x1, norm2_g, tm=tl["norm_tm"], transpose=True)
    s2, e2, thr, c = _peer_score(xn2_t, w_q_peer.T.astype(BF16), sub_keys.astype(BF16),
                                 tm=tl["peer_tm"])
    peer_t = _peer_mix(xn2_t, w_down.astype(BF16), w_up.T.astype(BF16), s2, e2, thr, c,
                       tm=tl["peer_tm"], eb=tl["peer_eb"])
    return _add_transposed(x1, peer_t, tm=tl["norm_tm"])


def kernel(x, norm1_g, w_in, q_norm_g, k_norm_g, sink_logits, conv_w, w_o_attn, w_o_conv,
           w_out, norm2_g, w_q_peer, sub_keys, w_down, w_up):
    b, s, d = x.shape
    x2 = x.reshape(b * s, d)
    for i in range(norm1_g.shape[0]):
        x2 = _layer(x2, s, norm1_g[i], w_in[i], q_norm_g[i], k_norm_g[i], sink_logits[i],
                    conv_w[i], w_o_attn[i], w_o_conv[i], w_out[i], norm2_g[i],
                    w_q_peer[i], sub_keys[i], w_down[i], w_up[i])
    return x2.reshape(b, s, d)
```

```python
import functools

import jax
import jax.numpy as jnp
from jax import lax
from jax.experimental import pallas as pl
from jax.experimental.pallas import tpu as pltpu

F32 = jnp.float32
BF16 = jnp.bfloat16

HEAD_DIM = 128
GQA_GROUP = 4
WINDOW = 128
BLOCK = 128
CONV_K = 3
PEER_HEADS = 8
N_KEYS = 128
PEER_KEY_DIM = 128
PEER_TOPK = 16
EPS = 1e-6
SUBLANES = 8
LANES = 128
BF16_COLS = 2 * LANES
MIX_ACC_VREGS = 32
VMEM_LIMIT = 56 * 1024 * 1024
PEER_MIX_VMEM_LIMIT = 60 * 1024 * 1024


def _params(sem, vmem_limit=VMEM_LIMIT):
    return pltpu.CompilerParams(dimension_semantics=sem, vmem_limit_bytes=vmem_limit)


def _rmsnorm_kernel(x_ref, g_ref, o_ref, *, transpose):
    x = x_ref[...]
    ms = jnp.mean(x * x, axis=-1, keepdims=True)
    y = x * lax.rsqrt(ms + EPS) * g_ref[...]
    if transpose:
        y = y.T
    o_ref[...] = y.astype(o_ref.dtype)


def _rmsnorm(x, g, *, tm, transpose=False):
    t, d = x.shape
    if transpose:
        out_shape = jax.ShapeDtypeStruct((d, t), BF16)
        out_spec = pl.BlockSpec((d, tm), lambda i: (0, i))
    else:
        out_shape = jax.ShapeDtypeStruct((t, d), BF16)
        out_spec = pl.BlockSpec((tm, d), lambda i: (i, 0))
    return pl.pallas_call(
        functools.partial(_rmsnorm_kernel, transpose=transpose),
        grid=(t // tm,),
        in_specs=[pl.BlockSpec((tm, d), lambda i: (i, 0)),
                  pl.BlockSpec((1, d), lambda i: (0, 0))],
        out_specs=out_spec,
        out_shape=out_shape,
        compiler_params=_params(("parallel",)),
        name="rmsnorm_t" if transpose else "rmsnorm",
    )(x, g.reshape(1, d))


def _mm_kernel(a_ref, b_ref, o_ref):
    o_ref[...] = jnp.dot(a_ref[...], b_ref[...],
                         preferred_element_type=F32).astype(o_ref.dtype)


def _matmul(a, b, *, tm, tn, out_dtype, name):
    m, k = a.shape
    n = b.shape[1]
    return pl.pallas_call(
        _mm_kernel,
        grid=(m // tm, n // tn),
        in_specs=[pl.BlockSpec((tm, k), lambda i, j: (i, 0)),
                  pl.BlockSpec((k, tn), lambda i, j: (0, j))],
        out_specs=pl.BlockSpec((tm, tn), lambda i, j: (i, j)),
        out_shape=jax.ShapeDtypeStruct((m, n), out_dtype),
        compiler_params=_params(("parallel", "arbitrary")),
        name=name,
    )(a, b)


def _mm_res_kernel(r_ref, a_ref, b_ref, o_ref):
    o_ref[...] = r_ref[...] + jnp.dot(a_ref[...], b_ref[...], preferred_element_type=F32)


def _matmul_residual(r, a, b, *, tm, tn):
    m, k = a.shape
    n = b.shape[1]
    return pl.pallas_call(
        _mm_res_kernel,
        grid=(m // tm, n // tn),
        in_specs=[pl.BlockSpec((tm, tn), lambda i, j: (i, j)),
                  pl.BlockSpec((tm, k), lambda i, j: (i, 0)),
                  pl.BlockSpec((k, tn), lambda i, j: (0, j))],
        out_specs=pl.BlockSpec((tm, tn), lambda i, j: (i, j)),
        out_shape=jax.ShapeDtypeStruct((m, n), F32),
        compiler_params=_params(("parallel", "arbitrary")),
        name="out_proj_residual",
    )(r, a, b)


def _merge_kernel(attn_ref, conv_ref, woa_ref, woc_ref, ga_ref, gc_ref, o_ref):
    a = jnp.dot(attn_ref[...], woa_ref[...], preferred_element_type=F32)
    c = jnp.dot(conv_ref[...], woc_ref[...], preferred_element_type=F32)
    ga = jax.nn.sigmoid(ga_ref[...].astype(F32))
    gc = jax.nn.sigmoid(gc_ref[...].astype(F32))
    o_ref[...] = (ga * a + gc * c).astype(o_ref.dtype)


def _merge(attn, conv, woa, woc, proj, *, off_ga, off_gc, tm, tn):
    t, qw = attn.shape
    cw = conv.shape[1]
    d = woa.shape[1]
    ga_blk, gc_blk = off_ga // tn, off_gc // tn
    return pl.pallas_call(
        _merge_kernel,
        grid=(t // tm, d // tn),
        in_specs=[pl.BlockSpec((tm, qw), lambda i, j: (i, 0)),
                  pl.BlockSpec((tm, cw), lambda i, j: (i, 0)),
                  pl.BlockSpec((qw, tn), lambda i, j: (0, j)),
                  pl.BlockSpec((cw, tn), lambda i, j: (0, j)),
                  pl.BlockSpec((tm, tn), lambda i, j: (i, ga_blk + j)),
                  pl.BlockSpec((tm, tn), lambda i, j: (i, gc_blk + j))],
        out_specs=pl.BlockSpec((tm, tn), lambda i, j: (i, j)),
        out_shape=jax.ShapeDtypeStruct((t, d), BF16),
        compiler_params=_params(("parallel", "arbitrary")),
        name="merge_proj",
    )(attn, conv, woa, woc, proj, proj)


def _head_rmsnorm(x, g):
    ms = jnp.mean(x * x, axis=-1, keepdims=True)
    return x * lax.rsqrt(ms + EPS) * g


def _attn_kernel(sink_ref, q_ref, kp_ref, kc_ref, kn_ref, vp_ref, vc_ref, vn_ref,
                 qg_ref, kg_ref, o_ref, *, nb, n_kv, n_heads):
    n = pl.program_id(0) % nb
    has_prev = n > 0
    has_next = n < nb - 1
    qi = lax.broadcasted_iota(jnp.int32, (BLOCK, 3 * BLOCK), 0)
    kj = lax.broadcasted_iota(jnp.int32, (BLOCK, 3 * BLOCK), 1)
    dist = jnp.abs(kj - BLOCK - qi)
    valid = ((dist <= WINDOW)
             & ((kj >= BLOCK) | has_prev)
             & ((kj < 2 * BLOCK) | has_next))
    distf = dist.astype(F32)
    qg = qg_ref[...]
    kg = kg_ref[...]
    scale = HEAD_DIM ** -0.5
    for kv in range(n_kv):
        cs = slice(kv * HEAD_DIM, (kv + 1) * HEAD_DIM)
        k = jnp.concatenate([kp_ref[:, cs], kc_ref[:, cs], kn_ref[:, cs]], axis=0)
        kb = _head_rmsnorm(k.astype(F32), kg).astype(BF16)
        v = jnp.concatenate([vp_ref[:, cs], vc_ref[:, cs], vn_ref[:, cs]], axis=0)
        for g in range(GQA_GROUP):
            hq = kv * GQA_GROUP + g
            hs = slice(hq * HEAD_DIM, (hq + 1) * HEAD_DIM)
            slope = 2.0 ** (-8.0 * (hq + 1) / n_heads)
            qb = _head_rmsnorm(q_ref[:, hs].astype(F32), qg).astype(BF16)
            s = lax.dot_general(qb, kb, (((1,), (1,)), ((), ())),
                                preferred_element_type=F32)
            s = s * scale - slope * distf
            s = jnp.where(valid, s, -jnp.inf)
            sink = sink_ref[hq]
            m = jnp.maximum(jnp.max(s, axis=-1, keepdims=True), sink)
            p = jnp.exp(s - m)
            denom = jnp.sum(p, axis=-1, keepdims=True) + jnp.exp(sink - m)
            probs = (p / denom).astype(BF16)
            o = jnp.dot(probs, v, preferred_element_type=F32)
            o_ref[:, hs] = o.astype(o_ref.dtype)


def _attention(proj, sink, qg, kg, *, seq, q_w, kv_w):
    t = proj.shape[0]
    nb = seq // BLOCK
    n_heads = q_w // HEAD_DIM
    n_kv = kv_w // HEAD_DIM
    k_blk = q_w // kv_w
    v_blk = k_blk + 1

    def prev_row(r):
        return jnp.where(r % nb > 0, r - 1, r)

    def next_row(r):
        return jnp.where(r % nb < nb - 1, r + 1, r)

    kv_specs = [pl.BlockSpec((BLOCK, kv_w), lambda r, c=c, f=f: (f(r), c))
                for c in (k_blk, v_blk) for f in (prev_row, lambda r: r, next_row)]
    return pl.pallas_call(
        functools.partial(_attn_kernel, nb=nb, n_kv=n_kv, n_heads=n_heads),
        grid=(t // BLOCK,),
        in_specs=[pl.BlockSpec(memory_space=pltpu.SMEM),
                  pl.BlockSpec((BLOCK, q_w), lambda r: (r, 0))]
                 + kv_specs
                 + [pl.BlockSpec((1, HEAD_DIM), lambda r: (0, 0)),
                    pl.BlockSpec((1, HEAD_DIM), lambda r: (0, 0))],
        out_specs=pl.BlockSpec((BLOCK, q_w), lambda r: (r, 0)),
        out_shape=jax.ShapeDtypeStruct((t, q_w), BF16),
        compiler_params=_params(("parallel",)),
        name="banded_attention",
    )(sink, proj, proj, proj, proj, proj, proj, proj,
      qg.reshape(1, HEAD_DIM), kg.reshape(1, HEAD_DIM))


def _conv_kernel(h_ref, b_ref, c_ref, hp_ref, cp_ref, hn_ref, cn_ref, w_ref, o_ref,
                 *, tiles_per_seq):
    i = pl.program_id(0)
    tm = h_ref.shape[0]
    u = c_ref[...].astype(F32) * h_ref[...].astype(F32)
    first = (i % tiles_per_seq) == 0
    last = (i % tiles_per_seq) == tiles_per_seq - 1
    u_prev = (cp_ref[SUBLANES - 1:SUBLANES, :].astype(F32)
              * hp_ref[SUBLANES - 1:SUBLANES, :].astype(F32))
    u_next = cn_ref[0:1, :].astype(F32) * hn_ref[0:1, :].astype(F32)
    u_prev = jnp.where(first, 0.0, u_prev)
    u_next = jnp.where(last, 0.0, u_next)
    row = lax.broadcasted_iota(jnp.int32, u.shape, 0)
    u_m1 = jnp.where(row == 0, u_prev, pltpu.roll(u, 1, axis=0))
    u_p1 = jnp.where(row == tm - 1, u_next, pltpu.roll(u, tm - 1, axis=0))
    w = w_ref[...]
    y = w[0:1, :] * u_m1 + w[1:2, :] * u + w[2:3, :] * u_p1
    o_ref[...] = (b_ref[...].astype(F32) * y).astype(o_ref.dtype)


def _short_conv(proj, conv_w, *, seq, off_h, conv_width, tm):
    t = proj.shape[0]
    cw = conv_width // 2
    nc = conv_width // cw
    hb, bb, cb = off_h // cw, (off_h + conv_width) // cw, (off_h + 2 * conv_width) // cw
    rows8 = tm // SUBLANES
    last8 = t // SUBLANES - 1

    def prev8(i):
        return jnp.maximum(i * rows8 - 1, 0)

    def next8(i):
        return jnp.minimum((i + 1) * rows8, last8)

    w_pad = jnp.zeros((SUBLANES, conv_width), F32).at[:CONV_K].set(conv_w)
    return pl.pallas_call(
        functools.partial(_conv_kernel, tiles_per_seq=seq // tm),
        grid=(t // tm, nc),
        in_specs=[pl.BlockSpec((tm, cw), lambda i, j: (i, hb + j)),
                  pl.BlockSpec((tm, cw), lambda i, j: (i, bb + j)),
                  pl.BlockSpec((tm, cw), lambda i, j: (i, cb + j)),
                  pl.BlockSpec((SUBLANES, cw), lambda i, j: (prev8(i), hb + j)),
                  pl.BlockSpec((SUBLANES, cw), lambda i, j: (prev8(i), cb + j)),
                  pl.BlockSpec((SUBLANES, cw), lambda i, j: (next8(i), hb + j)),
                  pl.BlockSpec((SUBLANES, cw), lambda i, j: (next8(i), cb + j)),
                  pl.BlockSpec((SUBLANES, cw), lambda i, j: (0, j))],
        out_specs=pl.BlockSpec((tm, cw), lambda i, j: (i, j)),
        out_shape=jax.ShapeDtypeStruct((t, conv_width), BF16),
        compiler_params=_params(("parallel", "arbitrary")),
        name="short_conv",
    )(proj, proj, proj, proj, proj, proj, proj, w_pad)


def _extract_top(work, count):
    n = work.shape[0]
    iota = lax.broadcasted_iota(jnp.int32, work.shape, 0).astype(F32)
    rank = jnp.full(work.shape, float(count), F32)
    vals = []
    for r in range(count):
        m = jnp.max(work, axis=0, keepdims=True)
        vals.append(m)
        first = jnp.min(jnp.where(work == m, iota, float(n)), axis=0, keepdims=True)
        sel = iota == first
        work = jnp.where(sel, -jnp.inf, work)
        rank = jnp.where(sel, float(r), rank)
    return vals, rank


def _peer_score_kernel(xt_ref, wq_ref, keys_ref, r2_ref, e2_ref, cnt_ref, c_ref):
    qt = jnp.dot(wq_ref[...], xt_ref[...], preferred_element_type=F32)
    q1 = qt[:PEER_KEY_DIM].astype(BF16)
    q2 = qt[PEER_KEY_DIM:].astype(BF16)
    s1 = jnp.dot(keys_ref[0, 0], q1, preferred_element_type=F32)
    s2 = jnp.dot(keys_ref[0, 1], q2, preferred_element_type=F32)
    tm = s1.shape[1]
    v1, rank1 = _extract_top(s1, PEER_TOPK)
    v2, rank2 = _extract_top(s2, PEER_TOPK)

    v2_all = jnp.concatenate(v2, axis=0)
    v2_low = v2_all[:SUBLANES]
    row = lax.broadcasted_iota(jnp.int32, (SUBLANES, tm), 0)
    pieces = [v1[0] + v2_all]
    for a in range(1, SUBLANES):
        nb = PEER_TOPK // (a + 1)
        piece = v1[a] + v2_low
        pieces.append(piece if nb >= SUBLANES else jnp.where(row < nb, piece, -jnp.inf))
    pieces.append(jnp.concatenate(v1[SUBLANES:], axis=0) + v2[0])
    cand = jnp.concatenate(pieces, axis=0)
    top, cand_rank = _extract_top(cand, PEER_TOPK)
    chosen = jnp.where(cand_rank < PEER_TOPK, 1.0, 0.0)

    cnt_a = [jnp.sum(chosen[:PEER_TOPK], axis=0, keepdims=True)]
    for a in range(1, SUBLANES):
        lo = PEER_TOPK + (a - 1) * SUBLANES
        cnt_a.append(jnp.sum(chosen[lo:lo + SUBLANES], axis=0, keepdims=True))
    lo = PEER_TOPK + (SUBLANES - 1) * SUBLANES
    cnt_a += [chosen[lo + k:lo + k + 1] for k in range(PEER_TOPK - SUBLANES)]
    cnt = jnp.zeros_like(s1)
    for a in range(PEER_TOPK):
        cnt = jnp.where(rank1 == float(a), cnt_a[a], cnt)

    m1, m2 = v1[0], v2[0]
    denom = jnp.exp(top[0] - (m1 + m2))
    for r in range(1, PEER_TOPK):
        denom = denom + jnp.exp(top[r] - (m1 + m2))
    r2_ref[...] = rank2.astype(r2_ref.dtype)
    e2_ref[...] = jnp.exp(s2 - m2).astype(e2_ref.dtype)
    cnt_ref[:, 0] = cnt.reshape(N_KEYS // SUBLANES, SUBLANES, tm)
    c_ref[:, 0] = (jnp.exp(s1 - m1) / denom).reshape(N_KEYS // SUBLANES, SUBLANES, tm)


def _peer_score(xt, wq_t, keys, *, tm):
    d, t = xt.shape
    qd = 2 * PEER_KEY_DIM
    groups = N_KEYS // SUBLANES
    dense = jax.ShapeDtypeStruct((PEER_HEADS * N_KEYS, t), BF16)
    grouped = jax.ShapeDtypeStruct((groups, PEER_HEADS, SUBLANES, t), F32)
    dense_spec = pl.BlockSpec((N_KEYS, tm), lambda i, h: (h, i))
    grouped_spec = pl.BlockSpec((groups, 1, SUBLANES, tm), lambda i, h: (0, h, 0, i))
    return pl.pallas_call(
        _peer_score_kernel,
        grid=(t // tm, PEER_HEADS),
        in_specs=[pl.BlockSpec((d, tm), lambda i, h: (0, i)),
                  pl.BlockSpec((qd, d), lambda i, h: (h, 0)),
                  pl.BlockSpec((1, 2, N_KEYS, PEER_KEY_DIM), lambda i, h: (h, 0, 0, 0))],
        out_specs=[dense_spec, dense_spec, grouped_spec, grouped_spec],
        out_shape=[dense, dense, grouped, grouped],
        compiler_params=_params(("parallel", "arbitrary")),
        name="peer_score",
    )(xt, wq_t, keys)


def _gelu(x):
    return 0.5 * x * (1.0 + lax.erf(x * (2.0 ** -0.5)))


def _build_gate_weights(r2_ref, e2_ref, cnt_ref, c_ref, w_ref, sub, n_i):
    tm = w_ref.shape[1]
    groups_per_tile = SUBLANES // n_i
    chunk_rows = SUBLANES * MIX_ACC_VREGS // n_i
    for tl in range(tm // BF16_COLS):
        cols = slice(tl * BF16_COLS, (tl + 1) * BF16_COLS)
        for jc in range(N_KEYS // chunk_rows):
            rows = slice(jc * chunk_rows, (jc + 1) * chunk_rows)
            acc = [None] * n_i
            packed = (chunk_rows // SUBLANES, SUBLANES, BF16_COLS)
            for h in range(PEER_HEADS):
                hrows = slice(h * N_KEYS + rows.start, h * N_KEYS + rows.stop)
                r2c = r2_ref[hrows, cols].reshape(packed)
                e2c = e2_ref[hrows, cols].reshape(packed)
                for ii in range(n_i):
                    cnt_row = cnt_ref[0, h, ii:ii + 1, cols]
                    c_row = c_ref[0, h, ii:ii + 1, cols]
                    for alt in range(1, groups_per_tile):
                        r = alt * n_i + ii
                        cnt_row = jnp.where(sub == alt, cnt_ref[0, h, r:r + 1, cols], cnt_row)
                        c_row = jnp.where(sub == alt, c_ref[0, h, r:r + 1, cols], c_row)
                    cnt_b = jnp.broadcast_to(cnt_row, (SUBLANES, BF16_COLS)).astype(BF16)[None]
                    c_b = jnp.broadcast_to(c_row, (SUBLANES, BF16_COLS)).astype(BF16)[None]
                    term = jnp.where(r2c < cnt_b, e2c * c_b, jnp.zeros((), BF16))
                    acc[ii] = term if acc[ii] is None else acc[ii] + term
            for ii in range(n_i):
                r0 = ii * N_KEYS + jc * chunk_rows
                w_ref[r0:r0 + chunk_rows, cols] = acc[ii].reshape(chunk_rows, BF16_COLS)


def _peer_mix_kernel(xt_ref, wd_ref, wu_ref, r2_ref, e2_ref, cnt_ref, c_ref, o_ref, w_ref,
                     *, n_i):
    e = pl.program_id(1)

    @pl.when(e == 0)
    def _():
        o_ref[...] = jnp.zeros_like(o_ref)

    _build_gate_weights(r2_ref, e2_ref, cnt_ref, c_ref, w_ref, e % (SUBLANES // n_i), n_i)
    a = jnp.dot(wd_ref[...], xt_ref[...], preferred_element_type=F32)
    gated = _gelu(a).astype(BF16) * w_ref[...]
    o_ref[...] += jnp.dot(wu_ref[...], gated, preferred_element_type=F32)


def _peer_mix(xt, w_down, w_up_t, rank2, e2, cnt, c, *, tm, eb):
    d, t = xt.shape
    ne = w_down.shape[0]
    n_i = eb // N_KEYS
    groups_per_tile = SUBLANES // n_i
    once = pl.Buffered(1)
    dense_spec = pl.BlockSpec((PEER_HEADS * N_KEYS, tm), lambda i, e: (0, i),
                              pipeline_mode=once)
    grouped_spec = pl.BlockSpec((1, PEER_HEADS, SUBLANES, tm),
                                lambda i, e: (e // groups_per_tile, 0, 0, i))
    return pl.pallas_call(
        functools.partial(_peer_mix_kernel, n_i=n_i),
        grid=(t // tm, ne // eb),
        in_specs=[pl.BlockSpec((d, tm), lambda i, e: (0, i), pipeline_mode=once),
                  pl.BlockSpec((eb, d), lambda i, e: (e, 0)),
                  pl.BlockSpec((d, eb), lambda i, e: (0, e)),
                  dense_spec, dense_spec, grouped_spec, grouped_spec],
        out_specs=pl.BlockSpec((d, tm), lambda i, e: (0, i), pipeline_mode=once),
        out_shape=jax.ShapeDtypeStruct((d, t), F32),
        scratch_shapes=[pltpu.VMEM((eb, tm), BF16)],
        compiler_params=_params(("parallel", "arbitrary"), PEER_MIX_VMEM_LIMIT),
        name="peer_mix",
    )(xt, w_down, w_up_t, rank2, e2, cnt, c)


def _add_t_kernel(x_ref, pt_ref, o_ref):
    o_ref[...] = x_ref[...] + pt_ref[...].T


def _add_transposed(x, pt, *, tm):
    t, d = x.shape
    return pl.pallas_call(
        _add_t_kernel,
        grid=(t // tm,),
        in_specs=[pl.BlockSpec((tm, d), lambda i: (i, 0)),
                  pl.BlockSpec((d, tm), lambda i: (0, i))],
        out_specs=pl.BlockSpec((tm, d), lambda i: (i, 0)),
        out_shape=jax.ShapeDtypeStruct((t, d), F32),
        compiler_params=_params(("parallel",)),
        name="add_transposed",
    )(x, pt)


def _tiles(t, d, in_w):
    def pick(n, cap):
        b = cap
        while n % b:
            b //= 2
        return b
    return dict(
        norm_tm=pick(t, 256),
        mm_tm=pick(t, 1024),
        mm_tn=pick(d // 4, 1024),
        conv_tm=pick(t, 512),
        peer_tm=pick(t, 512),
        score_tm=pick(t, 1024),
        peer_eb=1024,
    )


def _layer(x2, seq, norm1_g, w_in, q_norm_g, k_norm_g, sink_logits, conv_w, w_o_attn,
           w_o_conv, w_out, norm2_g, w_q_peer, sub_keys, w_down, w_up):
    t, d = x2.shape
    in_w = w_in.shape[1]
    q_w = w_o_attn.shape[0]
    kv_w = q_w // GQA_GROUP
    conv_width = w_o_conv.shape[0]
    off_h = q_w + 2 * kv_w
    off_ga = off_h + 3 * conv_width
    off_gc = off_ga + d
    tl = _tiles(t, d, in_w)

    xn = _rmsnorm(x2, norm1_g, tm=tl["norm_tm"])
    proj = _matmul(xn, w_in.astype(BF16), tm=tl["mm_tm"], tn=tl["mm_tn"],
                   out_dtype=BF16, name="in_proj")
    attn = _attention(proj, sink_logits.astype(F32), q_norm_g, k_norm_g,
                      seq=seq, q_w=q_w, kv_w=kv_w)
    conv = _short_conv(proj, conv_w, seq=seq, off_h=off_h, conv_width=conv_width,
                       tm=tl["conv_tm"])
    merged = _merge(attn, conv, w_o_attn.astype(BF16), w_o_conv.astype(BF16), proj,
                    off_ga=off_ga, off_gc=off_gc, tm=tl["mm_tm"], tn=tl["mm_tn"])
    x1 = _matmul_residual(x2, merged, w_out.astype(BF16), tm=tl["mm_tm"], tn=tl["mm_tn"])

    xn2_t = _rmsnorm(x1, norm2_g, tm=tl["norm_tm"], transpose=True)
    rank2, e2, cnt, c = _peer_score(xn2_t, w_q_peer.T.astype(BF16), sub_keys.astype(BF16),
                                    tm=tl["score_tm"])
    peer_t = _peer_mix(xn2_t, w_down.astype(BF16), w_up.T.astype(BF16), rank2, e2, cnt, c,
                       tm=tl["peer_tm"], eb=tl["peer_eb"])
    return _add_transposed(x1, peer_t, tm=tl["norm_tm"])


def kernel(x, norm1_g, w_in, q_norm_g, k_norm_g, sink_logits, conv_w, w_o_attn, w_o_conv,
           w_out, norm2_g, w_q_peer, sub_keys, w_down, w_up):
    b, s, d = x.shape
    x2 = x.reshape(b * s, d)
    for i in range(norm1_g.shape[0]):
        x2 = _layer(x2, s, norm1_g[i], w_in[i], q_norm_g[i], k_norm_g[i], sink_logits[i],
                    conv_w[i], w_o_attn[i], w_o_conv[i], w_out[i], norm2_g[i],
                    w_q_peer[i], sub_keys[i], w_down[i], w_up[i])
    return x2.reshape(b, s, d)
```

```python
import functools

import jax
import jax.numpy as jnp
from jax import lax
from jax.experimental import pallas as pl
from jax.experimental.pallas import tpu as pltpu

F32 = jnp.float32
BF16 = jnp.bfloat16

HEAD_DIM = 128
GQA_GROUP = 4
WINDOW = 128
BLOCK = 128
CONV_K = 3
PEER_HEADS = 8
N_KEYS = 128
PEER_KEY_DIM = 128
PEER_TOPK = 16
EPS = 1e-6
SUBLANES = 8
LANES = 128
BF16_COLS = 2 * LANES
MIX_ACC_VREGS = 32
VMEM_LIMIT = 56 * 1024 * 1024
PEER_MIX_VMEM_LIMIT = 60 * 1024 * 1024


def _params(sem, vmem_limit=VMEM_LIMIT):
    return pltpu.CompilerParams(dimension_semantics=sem, vmem_limit_bytes=vmem_limit)


def _rmsnorm_kernel(x_ref, g_ref, o_ref, *, transpose):
    x = x_ref[...]
    ms = jnp.mean(x * x, axis=-1, keepdims=True)
    y = x * lax.rsqrt(ms + EPS) * g_ref[...]
    if transpose:
        y = y.T
    o_ref[...] = y.astype(o_ref.dtype)


def _rmsnorm(x, g, *, tm, transpose=False):
    t, d = x.shape
    if transpose:
        out_shape = jax.ShapeDtypeStruct((d, t), BF16)
        out_spec = pl.BlockSpec((d, tm), lambda i: (0, i))
    else:
        out_shape = jax.ShapeDtypeStruct((t, d), BF16)
        out_spec = pl.BlockSpec((tm, d), lambda i: (i, 0))
    return pl.pallas_call(
        functools.partial(_rmsnorm_kernel, transpose=transpose),
        grid=(t // tm,),
        in_specs=[pl.BlockSpec((tm, d), lambda i: (i, 0)),
                  pl.BlockSpec((1, d), lambda i: (0, 0))],
        out_specs=out_spec,
        out_shape=out_shape,
        compiler_params=_params(("parallel",)),
        name="rmsnorm_t" if transpose else "rmsnorm",
    )(x, g.reshape(1, d))


def _mm_kernel(a_ref, b_ref, o_ref):
    o_ref[...] = jnp.dot(a_ref[...], b_ref[...],
                         preferred_element_type=F32).astype(o_ref.dtype)


def _matmul(a, b, *, tm, tn, out_dtype, name):
    m, k = a.shape
    n = b.shape[1]
    return pl.pallas_call(
        _mm_kernel,
        grid=(m // tm, n // tn),
        in_specs=[pl.BlockSpec((tm, k), lambda i, j: (i, 0)),
                  pl.BlockSpec((k, tn), lambda i, j: (0, j))],
        out_specs=pl.BlockSpec((tm, tn), lambda i, j: (i, j)),
        out_shape=jax.ShapeDtypeStruct((m, n), out_dtype),
        compiler_params=_params(("parallel", "arbitrary")),
        name=name,
    )(a, b)


def _mm_res_kernel(r_ref, a_ref, b_ref, o_ref):
    o_ref[...] = r_ref[...] + jnp.dot(a_ref[...], b_ref[...], preferred_element_type=F32)


def _matmul_residual(r, a, b, *, tm, tn):
    m, k = a.shape
    n = b.shape[1]
    return pl.pallas_call(
        _mm_res_kernel,
        grid=(m // tm, n // tn),
        in_specs=[pl.BlockSpec((tm, tn), lambda i, j: (i, j)),
                  pl.BlockSpec((tm, k), lambda i, j: (i, 0)),
                  pl.BlockSpec((k, tn), lambda i, j: (0, j))],
        out_specs=pl.BlockSpec((tm, tn), lambda i, j: (i, j)),
        out_shape=jax.ShapeDtypeStruct((m, n), F32),
        compiler_params=_params(("parallel", "arbitrary")),
        name="out_proj_residual",
    )(r, a, b)


def _merge_kernel(attn_ref, conv_ref, woa_ref, woc_ref, ga_ref, gc_ref, o_ref):
    a = jnp.dot(attn_ref[...], woa_ref[...], preferred_element_type=F32)
    c = jnp.dot(conv_ref[...], woc_ref[...], preferred_element_type=F32)
    ga = jax.nn.sigmoid(ga_ref[...].astype(F32))
    gc = jax.nn.sigmoid(gc_ref[...].astype(F32))
    o_ref[...] = (ga * a + gc * c).astype(o_ref.dtype)


def _merge(attn, conv, woa, woc, proj, *, off_ga, off_gc, tm, tn):
    t, qw = attn.shape
    cw = conv.shape[1]
    d = woa.shape[1]
    ga_blk, gc_blk = off_ga // tn, off_gc // tn
    return pl.pallas_call(
        _merge_kernel,
        grid=(t // tm, d // tn),
        in_specs=[pl.BlockSpec((tm, qw), lambda i, j: (i, 0)),
                  pl.BlockSpec((tm, cw), lambda i, j: (i, 0)),
                  pl.BlockSpec((qw, tn), lambda i, j: (0, j)),
                  pl.BlockSpec((cw, tn), lambda i, j: (0, j)),
                  pl.BlockSpec((tm, tn), lambda i, j: (i, ga_blk + j)),
                  pl.BlockSpec((tm, tn), lambda i, j: (i, gc_blk + j))],
        out_specs=pl.BlockSpec((tm, tn), lambda i, j: (i, j)),
        out_shape=jax.ShapeDtypeStruct((t, d), BF16),
        compiler_params=_params(("parallel", "arbitrary")),
        name="merge_proj",
    )(attn, conv, woa, woc, proj, proj)


def _head_rmsnorm(x, g):
    ms = jnp.mean(x * x, axis=-1, keepdims=True)
    return x * lax.rsqrt(ms + EPS) * g


def _attn_kernel(sink_ref, q_ref, kp_ref, kc_ref, kn_ref, vp_ref, vc_ref, vn_ref,
                 qg_ref, kg_ref, o_ref, *, nb, n_kv, n_heads):
    n = pl.program_id(0) % nb
    has_prev = n > 0
    has_next = n < nb - 1
    qi = lax.broadcasted_iota(jnp.int32, (BLOCK, 3 * BLOCK), 0)
    kj = lax.broadcasted_iota(jnp.int32, (BLOCK, 3 * BLOCK), 1)
    dist = jnp.abs(kj - BLOCK - qi)
    valid = ((dist <= WINDOW)
             & ((kj >= BLOCK) | has_prev)
             & ((kj < 2 * BLOCK) | has_next))
    distf = dist.astype(F32)
    qg = qg_ref[...]
    kg = kg_ref[...]
    scale = HEAD_DIM ** -0.5
    heads = range(n_heads)
    cols = [slice(h * HEAD_DIM, (h + 1) * HEAD_DIM) for h in heads]
    kbs = [_head_rmsnorm(jnp.concatenate([kp_ref[:, cols[kv]], kc_ref[:, cols[kv]],
                                          kn_ref[:, cols[kv]]], axis=0).astype(F32),
                         kg).astype(BF16) for kv in range(n_kv)]
    vs = [jnp.concatenate([vp_ref[:, cols[kv]], vc_ref[:, cols[kv]], vn_ref[:, cols[kv]]],
                          axis=0) for kv in range(n_kv)]
    qbs = [_head_rmsnorm(q_ref[:, cols[h]].astype(F32), qg).astype(BF16) for h in heads]
    ss = [lax.dot_general(qbs[h], kbs[h // GQA_GROUP], (((1,), (1,)), ((), ())),
                          preferred_element_type=F32) for h in heads]
    ss = [jnp.where(valid, ss[h] * scale - 2.0 ** (-8.0 * (h + 1) / n_heads) * distf, -jnp.inf)
          for h in heads]
    sinks = [sink_ref[h] for h in heads]
    ms = [jnp.maximum(jnp.max(ss[h], axis=-1, keepdims=True), sinks[h]) for h in heads]
    ps = [jnp.exp(ss[h] - ms[h]) for h in heads]
    invs = [1.0 / (jnp.sum(ps[h], axis=-1, keepdims=True) + jnp.exp(sinks[h] - ms[h]))
            for h in heads]
    for h in heads:
        probs = (ps[h] * invs[h]).astype(BF16)
        o = jnp.dot(probs, vs[h // GQA_GROUP], preferred_element_type=F32)
        o_ref[:, cols[h]] = o.astype(o_ref.dtype)


def _attention(proj, sink, qg, kg, *, seq, q_w, kv_w):
    t = proj.shape[0]
    nb = seq // BLOCK
    n_heads = q_w // HEAD_DIM
    n_kv = kv_w // HEAD_DIM
    k_blk = q_w // kv_w
    v_blk = k_blk + 1

    def prev_row(r):
        return jnp.where(r % nb > 0, r - 1, r)

    def next_row(r):
        return jnp.where(r % nb < nb - 1, r + 1, r)

    kv_specs = [pl.BlockSpec((BLOCK, kv_w), lambda r, c=c, f=f: (f(r), c))
                for c in (k_blk, v_blk) for f in (prev_row, lambda r: r, next_row)]
    return pl.pallas_call(
        functools.partial(_attn_kernel, nb=nb, n_kv=n_kv, n_heads=n_heads),
        grid=(t // BLOCK,),
        in_specs=[pl.BlockSpec(memory_space=pltpu.SMEM),
                  pl.BlockSpec((BLOCK, q_w), lambda r: (r, 0))]
                 + kv_specs
                 + [pl.BlockSpec((1, HEAD_DIM), lambda r: (0, 0)),
                    pl.BlockSpec((1, HEAD_DIM), lambda r: (0, 0))],
        out_specs=pl.BlockSpec((BLOCK, q_w), lambda r: (r, 0)),
        out_shape=jax.ShapeDtypeStruct((t, q_w), BF16),
        compiler_params=_params(("parallel",)),
        name="banded_attention",
    )(sink, proj, proj, proj, proj, proj, proj, proj,
      qg.reshape(1, HEAD_DIM), kg.reshape(1, HEAD_DIM))


def _conv_kernel(h_ref, b_ref, c_ref, hp_ref, cp_ref, hn_ref, cn_ref, w_ref, o_ref,
                 *, tiles_per_seq):
    i = pl.program_id(0)
    tm = h_ref.shape[0]
    u = c_ref[...].astype(F32) * h_ref[...].astype(F32)
    first = (i % tiles_per_seq) == 0
    last = (i % tiles_per_seq) == tiles_per_seq - 1
    u_prev = (cp_ref[SUBLANES - 1:SUBLANES, :].astype(F32)
              * hp_ref[SUBLANES - 1:SUBLANES, :].astype(F32))
    u_next = cn_ref[0:1, :].astype(F32) * hn_ref[0:1, :].astype(F32)
    u_prev = jnp.where(first, 0.0, u_prev)
    u_next = jnp.where(last, 0.0, u_next)
    row = lax.broadcasted_iota(jnp.int32, u.shape, 0)
    u_m1 = jnp.where(row == 0, u_prev, pltpu.roll(u, 1, axis=0))
    u_p1 = jnp.where(row == tm - 1, u_next, pltpu.roll(u, tm - 1, axis=0))
    w = w_ref[...]
    y = w[0:1, :] * u_m1 + w[1:2, :] * u + w[2:3, :] * u_p1
    o_ref[...] = (b_ref[...].astype(F32) * y).astype(o_ref.dtype)


def _short_conv(proj, conv_w, *, seq, off_h, conv_width, tm):
    t = proj.shape[0]
    cw = conv_width // 2
    nc = conv_width // cw
    hb, bb, cb = off_h // cw, (off_h + conv_width) // cw, (off_h + 2 * conv_width) // cw
    rows8 = tm // SUBLANES
    last8 = t // SUBLANES - 1

    def prev8(i):
        return jnp.maximum(i * rows8 - 1, 0)

    def next8(i):
        return jnp.minimum((i + 1) * rows8, last8)

    w_pad = jnp.zeros((SUBLANES, conv_width), F32).at[:CONV_K].set(conv_w)
    return pl.pallas_call(
        functools.partial(_conv_kernel, tiles_per_seq=seq // tm),
        grid=(t // tm, nc),
        in_specs=[pl.BlockSpec((tm, cw), lambda i, j: (i, hb + j)),
                  pl.BlockSpec((tm, cw), lambda i, j: (i, bb + j)),
                  pl.BlockSpec((tm, cw), lambda i, j: (i, cb + j)),
                  pl.BlockSpec((SUBLANES, cw), lambda i, j: (prev8(i), hb + j)),
                  pl.BlockSpec((SUBLANES, cw), lambda i, j: (prev8(i), cb + j)),
                  pl.BlockSpec((SUBLANES, cw), lambda i, j: (next8(i), hb + j)),
                  pl.BlockSpec((SUBLANES, cw), lambda i, j: (next8(i), cb + j)),
                  pl.BlockSpec((SUBLANES, cw), lambda i, j: (0, j))],
        out_specs=pl.BlockSpec((tm, cw), lambda i, j: (i, j)),
        out_shape=jax.ShapeDtypeStruct((t, conv_width), BF16),
        compiler_params=_params(("parallel", "arbitrary")),
        name="short_conv",
    )(proj, proj, proj, proj, proj, proj, proj, w_pad)


def _extract_top(work, count):
    n = work.shape[0]
    iota = lax.broadcasted_iota(jnp.int32, work.shape, 0).astype(F32)
    rank = jnp.full(work.shape, float(count), F32)
    vals = []
    for r in range(count):
        m = jnp.max(work, axis=0, keepdims=True)
        vals.append(m)
        first = jnp.min(jnp.where(work == m, iota, float(n)), axis=0, keepdims=True)
        sel = iota == first
        work = jnp.where(sel, -jnp.inf, work)
        rank = jnp.where(sel, float(r), rank)
    return vals, rank


def _peer_score_kernel(xt_ref, wq_ref, keys_ref, r2_ref, e2_ref, cnt_ref, c_ref):
    qt = jnp.dot(wq_ref[...], xt_ref[...], preferred_element_type=F32)
    q1 = qt[:PEER_KEY_DIM].astype(BF16)
    q2 = qt[PEER_KEY_DIM:].astype(BF16)
    s1 = jnp.dot(keys_ref[0, 0], q1, preferred_element_type=F32)
    s2 = jnp.dot(keys_ref[0, 1], q2, preferred_element_type=F32)
    tm = s1.shape[1]
    v12, rank12 = _extract_top(jnp.concatenate([s1, s2], axis=1), PEER_TOPK)
    v1, v2 = [v[:, :tm] for v in v12], [v[:, tm:] for v in v12]
    rank1, rank2 = rank12[:, :tm], rank12[:, tm:]

    v2_all = jnp.concatenate(v2, axis=0)
    v2_low = v2_all[:SUBLANES]
    row = lax.broadcasted_iota(jnp.int32, (SUBLANES, tm), 0)
    pieces = [v1[0] + v2_all]
    for a in range(1, SUBLANES):
        nb = PEER_TOPK // (a + 1)
        piece = v1[a] + v2_low
        pieces.append(piece if nb >= SUBLANES else jnp.where(row < nb, piece, -jnp.inf))
    pieces.append(jnp.concatenate(v1[SUBLANES:], axis=0) + v2[0])
    cand = jnp.concatenate(pieces, axis=0)
    top, cand_rank = _extract_top(cand, PEER_TOPK)
    chosen = jnp.where(cand_rank < PEER_TOPK, 1.0, 0.0)

    cnt_a = [jnp.sum(chosen[:PEER_TOPK], axis=0, keepdims=True)]
    for a in range(1, SUBLANES):
        lo = PEER_TOPK + (a - 1) * SUBLANES
        cnt_a.append(jnp.sum(chosen[lo:lo + SUBLANES], axis=0, keepdims=True))
    lo = PEER_TOPK + (SUBLANES - 1) * SUBLANES
    cnt_a += [chosen[lo + k:lo + k + 1] for k in range(PEER_TOPK - SUBLANES)]
    cnt = jnp.zeros_like(s1)
    for a in range(PEER_TOPK):
        cnt = jnp.where(rank1 == float(a), cnt_a[a], cnt)

    m1, m2 = v1[0], v2[0]
    denom = jnp.exp(top[0] - (m1 + m2))
    for r in range(1, PEER_TOPK):
        denom = denom + jnp.exp(top[r] - (m1 + m2))
    r2_ref[...] = rank2.astype(r2_ref.dtype)
    e2_ref[...] = jnp.exp(s2 - m2).astype(e2_ref.dtype)
    cnt_ref[:, 0] = cnt.reshape(N_KEYS // SUBLANES, SUBLANES, tm)
    c_ref[:, 0] = (jnp.exp(s1 - m1) / denom).reshape(N_KEYS // SUBLANES, SUBLANES, tm)


def _peer_score(xt, wq_t, keys, *, tm):
    d, t = xt.shape
    qd = 2 * PEER_KEY_DIM
    groups = N_KEYS // SUBLANES
    dense = jax.ShapeDtypeStruct((PEER_HEADS * N_KEYS, t), BF16)
    grouped = jax.ShapeDtypeStruct((groups, PEER_HEADS, SUBLANES, t), F32)
    dense_spec = pl.BlockSpec((N_KEYS, tm), lambda i, h: (h, i))
    grouped_spec = pl.BlockSpec((groups, 1, SUBLANES, tm), lambda i, h: (0, h, 0, i))
    return pl.pallas_call(
        _peer_score_kernel,
        grid=(t // tm, PEER_HEADS),
        in_specs=[pl.BlockSpec((d, tm), lambda i, h: (0, i)),
                  pl.BlockSpec((qd, d), lambda i, h: (h, 0)),
                  pl.BlockSpec((1, 2, N_KEYS, PEER_KEY_DIM), lambda i, h: (h, 0, 0, 0))],
        out_specs=[dense_spec, dense_spec, grouped_spec, grouped_spec],
        out_shape=[dense, dense, grouped, grouped],
        compiler_params=_params(("parallel", "arbitrary")),
        name="peer_score",
    )(xt, wq_t, keys)


def _gelu(x):
    return 0.5 * x * (1.0 + lax.erf(x * (2.0 ** -0.5)))


def _build_gate_weights(r2_ref, e2_ref, cnt_ref, c_ref, w_ref, sub, n_i):
    tm = w_ref.shape[1]
    groups_per_tile = SUBLANES // n_i
    chunk_rows = SUBLANES * MIX_ACC_VREGS // n_i
    for tl in range(tm // BF16_COLS):
        cols = slice(tl * BF16_COLS, (tl + 1) * BF16_COLS)
        for jc in range(N_KEYS // chunk_rows):
            rows = slice(jc * chunk_rows, (jc + 1) * chunk_rows)
            acc = [None] * n_i
            packed = (chunk_rows // SUBLANES, SUBLANES, BF16_COLS)
            for h in range(PEER_HEADS):
                hrows = slice(h * N_KEYS + rows.start, h * N_KEYS + rows.stop)
                r2c = r2_ref[hrows, cols].reshape(packed)
                e2c = e2_ref[hrows, cols].reshape(packed)
                for ii in range(n_i):
                    cnt_row = cnt_ref[0, h, ii:ii + 1, cols]
                    c_row = c_ref[0, h, ii:ii + 1, cols]
                    for alt in range(1, groups_per_tile):
                        r = alt * n_i + ii
                        cnt_row = jnp.where(sub == alt, cnt_ref[0, h, r:r + 1, cols], cnt_row)
                        c_row = jnp.where(sub == alt, c_ref[0, h, r:r + 1, cols], c_row)
                    cnt_b = jnp.broadcast_to(cnt_row, (SUBLANES, BF16_COLS)).astype(BF16)[None]
                    c_b = jnp.broadcast_to(c_row, (SUBLANES, BF16_COLS)).astype(BF16)[None]
                    term = jnp.where(r2c < cnt_b, e2c * c_b, jnp.zeros((), BF16))
                    acc[ii] = term if acc[ii] is None else acc[ii] + term
            for ii in range(n_i):
                r0 = ii * N_KEYS + jc * chunk_rows
                w_ref[r0:r0 + chunk_rows, cols] = acc[ii].reshape(chunk_rows, BF16_COLS)


def _peer_mix_kernel(xt_ref, wd_ref, wu_ref, r2_ref, e2_ref, cnt_ref, c_ref, o_ref, w_ref,
                     *, n_i):
    e = pl.program_id(1)

    @pl.when(e == 0)
    def _():
        o_ref[...] = jnp.zeros_like(o_ref)

    _build_gate_weights(r2_ref, e2_ref, cnt_ref, c_ref, w_ref, e % (SUBLANES // n_i), n_i)
    a = jnp.dot(wd_ref[...], xt_ref[...], preferred_element_type=F32)
    gated = _gelu(a).astype(BF16) * w_ref[...]
    o_ref[...] += jnp.dot(wu_ref[...], gated, preferred_element_type=F32)


def _peer_mix(xt, w_down, w_up_t, rank2, e2, cnt, c, *, tm, eb):
    d, t = xt.shape
    ne = w_down.shape[0]
    n_i = eb // N_KEYS
    groups_per_tile = SUBLANES // n_i
    once = pl.Buffered(1)
    dense_spec = pl.BlockSpec((PEER_HEADS * N_KEYS, tm), lambda i, e: (0, i),
                              pipeline_mode=once)
    grouped_spec = pl.BlockSpec((1, PEER_HEADS, SUBLANES, tm),
                                lambda i, e: (e // groups_per_tile, 0, 0, i))
    return pl.pallas_call(
        functools.partial(_peer_mix_kernel, n_i=n_i),
        grid=(t // tm, ne // eb),
        in_specs=[pl.BlockSpec((d, tm), lambda i, e: (0, i), pipeline_mode=once),
                  pl.BlockSpec((eb, d), lambda i, e: (e, 0)),
                  pl.BlockSpec((d, eb), lambda i, e: (0, e)),
                  dense_spec, dense_spec, grouped_spec, grouped_spec],
        out_specs=pl.BlockSpec((d, tm), lambda i, e: (0, i), pipeline_mode=once),
        out_shape=jax.ShapeDtypeStruct((d, t), F32),
        scratch_shapes=[pltpu.VMEM((eb, tm), BF16)],
        compiler_params=_params(("parallel", "arbitrary"), PEER_MIX_VMEM_LIMIT),
        name="peer_mix",
    )(xt, w_down, w_up_t, rank2, e2, cnt, c)


def _add_t_kernel(x_ref, pt_ref, o_ref):
    o_ref[...] = x_ref[...] + pt_ref[...].T


def _add_transposed(x, pt, *, tm):
    t, d = x.shape
    return pl.pallas_call(
        _add_t_kernel,
        grid=(t // tm,),
        in_specs=[pl.BlockSpec((tm, d), lambda i: (i, 0)),
                  pl.BlockSpec((d, tm), lambda i: (0, i))],
        out_specs=pl.BlockSpec((tm, d), lambda i: (i, 0)),
        out_shape=jax.ShapeDtypeStruct((t, d), F32),
        compiler_params=_params(("parallel",)),
        name="add_transposed",
    )(x, pt)


def _tiles(t, d, in_w):
    def pick(n, cap):
        b = cap
        while n % b:
            b //= 2
        return b
    return dict(
        norm_tm=pick(t, 256),
        mm_tm=pick(t, 1024),
        mm_tn=pick(d // 4, 1024),
        conv_tm=pick(t, 512),
        peer_tm=pick(t, 512),
        score_tm=pick(t, 1024),
        peer_eb=1024,
    )


def _layer(x2, seq, norm1_g, w_in, q_norm_g, k_norm_g, sink_logits, conv_w, w_o_attn,
           w_o_conv, w_out, norm2_g, w_q_peer, sub_keys, w_down, w_up):
    t, d = x2.shape
    in_w = w_in.shape[1]
    q_w = w_o_attn.shape[0]
    kv_w = q_w // GQA_GROUP
    conv_width = w_o_conv.shape[0]
    off_h = q_w + 2 * kv_w
    off_ga = off_h + 3 * conv_width
    off_gc = off_ga + d
    tl = _tiles(t, d, in_w)

    xn = _rmsnorm(x2, norm1_g, tm=tl["norm_tm"])
    proj = _matmul(xn, w_in.astype(BF16), tm=tl["mm_tm"], tn=tl["mm_tn"],
                   out_dtype=BF16, name="in_proj")
    attn = _attention(proj, sink_logits.astype(F32), q_norm_g, k_norm_g,
                      seq=seq, q_w=q_w, kv_w=kv_w)
    conv = _short_conv(proj, conv_w, seq=seq, off_h=off_h, conv_width=conv_width,
                       tm=tl["conv_tm"])
    merged = _merge(attn, conv, w_o_attn.astype(BF16), w_o_conv.astype(BF16), proj,
                    off_ga=off_ga, off_gc=off_gc, tm=tl["mm_tm"], tn=tl["mm_tn"])
    x1 = _matmul_residual(x2, merged, w_out.astype(BF16), tm=tl["mm_tm"], tn=tl["mm_tn"])

    xn2_t = _rmsnorm(x1, norm2_g, tm=tl["norm_tm"], transpose=True)
    rank2, e2, cnt, c = _peer_score(xn2_t, w_q_peer.astype(BF16).T, sub_keys.astype(BF16),
                                    tm=tl["score_tm"])
    peer_t = _peer_mix(xn2_t, w_down.astype(BF16), w_up.astype(BF16).T, rank2, e2, cnt, c,
                       tm=tl["peer_tm"], eb=tl["peer_eb"])
    return _add_transposed(x1, peer_t, tm=tl["norm_tm"])


def kernel(x, norm1_g, w_in, q_norm_g, k_norm_g, sink_logits, conv_w, w_o_attn, w_o_conv,
           w_out, norm2_g, w_q_peer, sub_keys, w_down, w_up):
    b, s, d = x.shape
    x2 = x.reshape(b * s, d)
    for i in range(norm1_g.shape[0]):
        x2 = _layer(x2, s, norm1_g[i], w_in[i], q_norm_g[i], k_norm_g[i], sink_logits[i],
                    conv_w[i], w_o_attn[i], w_o_conv[i], w_out[i], norm2_g[i],
                    w_q_peer[i], sub_keys[i], w_down[i], w_up[i])
    return x2.reshape(b, s, d)
```

```python
import functools

import jax
import jax.numpy as jnp
from jax import lax
from jax.experimental import pallas as pl
from jax.experimental.pallas import tpu as pltpu

F32 = jnp.float32
BF16 = jnp.bfloat16

HEAD_DIM = 128
GQA_GROUP = 4
WINDOW = 128
BLOCK = 128
CONV_K = 3
PEER_HEADS = 8
N_KEYS = 128
PEER_KEY_DIM = 128
PEER_TOPK = 16
EPS = 1e-6
SUBLANES = 8
LANES = 128
BF16_COLS = 2 * LANES
MIX_ACC_VREGS = 32
VMEM_LIMIT = 56 * 1024 * 1024
PEER_MIX_VMEM_LIMIT = 60 * 1024 * 1024


def _params(sem, vmem_limit=VMEM_LIMIT):
    return pltpu.CompilerParams(dimension_semantics=sem, vmem_limit_bytes=vmem_limit)


def _rmsnorm_kernel(x_ref, g_ref, o_ref, *, transpose):
    x = x_ref[...]
    ms = jnp.mean(x * x, axis=-1, keepdims=True)
    y = x * lax.rsqrt(ms + EPS) * g_ref[...]
    if transpose:
        y = y.T
    o_ref[...] = y.astype(o_ref.dtype)


def _rmsnorm(x, g, *, tm, transpose=False):
    t, d = x.shape
    if transpose:
        out_shape = jax.ShapeDtypeStruct((d, t), BF16)
        out_spec = pl.BlockSpec((d, tm), lambda i: (0, i))
    else:
        out_shape = jax.ShapeDtypeStruct((t, d), BF16)
        out_spec = pl.BlockSpec((tm, d), lambda i: (i, 0))
    return pl.pallas_call(
        functools.partial(_rmsnorm_kernel, transpose=transpose),
        grid=(t // tm,),
        in_specs=[pl.BlockSpec((tm, d), lambda i: (i, 0)),
                  pl.BlockSpec((1, d), lambda i: (0, 0))],
        out_specs=out_spec,
        out_shape=out_shape,
        compiler_params=_params(("parallel",)),
        name="rmsnorm_t" if transpose else "rmsnorm",
    )(x, g.reshape(1, d))


def _cast_t_kernel(x_ref, o_ref):
    o_ref[...] = x_ref[...].T.astype(o_ref.dtype)


def _cast_transpose(w, *, tr):
    r, c = w.shape
    return pl.pallas_call(
        _cast_t_kernel,
        grid=(r // tr,),
        in_specs=[pl.BlockSpec((tr, c), lambda i: (i, 0))],
        out_specs=pl.BlockSpec((c, tr), lambda i: (0, i)),
        out_shape=jax.ShapeDtypeStruct((c, r), BF16),
        compiler_params=_params(("parallel",)),
        name="cast_transpose",
    )(w)


def _mm_kernel(a_ref, b_ref, o_ref):
    o_ref[...] = jnp.dot(a_ref[...], b_ref[...],
                         preferred_element_type=F32).astype(o_ref.dtype)


def _matmul(a, b, *, tm, tn, out_dtype, name):
    m, k = a.shape
    n = b.shape[1]
    return pl.pallas_call(
        _mm_kernel,
        grid=(m // tm, n // tn),
        in_specs=[pl.BlockSpec((tm, k), lambda i, j: (i, 0)),
                  pl.BlockSpec((k, tn), lambda i, j: (0, j))],
        out_specs=pl.BlockSpec((tm, tn), lambda i, j: (i, j)),
        out_shape=jax.ShapeDtypeStruct((m, n), out_dtype),
        compiler_params=_params(("parallel", "arbitrary")),
        name=name,
    )(a, b)


def _mm_res_kernel(r_ref, a_ref, b_ref, o_ref):
    o_ref[...] = r_ref[...] + jnp.dot(a_ref[...], b_ref[...], preferred_element_type=F32)


def _matmul_residual(r, a, b, *, tm, tn):
    m, k = a.shape
    n = b.shape[1]
    return pl.pallas_call(
        _mm_res_kernel,
        grid=(m // tm, n // tn),
        in_specs=[pl.BlockSpec((tm, tn), lambda i, j: (i, j)),
                  pl.BlockSpec((tm, k), lambda i, j: (i, 0)),
                  pl.BlockSpec((k, tn), lambda i, j: (0, j))],
        out_specs=pl.BlockSpec((tm, tn), lambda i, j: (i, j)),
        out_shape=jax.ShapeDtypeStruct((m, n), F32),
        compiler_params=_params(("parallel", "arbitrary")),
        name="out_proj_residual",
    )(r, a, b)


def _merge_kernel(attn_ref, conv_ref, woa_ref, woc_ref, ga_ref, gc_ref, o_ref):
    a = jnp.dot(attn_ref[...], woa_ref[...], preferred_element_type=F32)
    c = jnp.dot(conv_ref[...], woc_ref[...], preferred_element_type=F32)
    ga = jax.nn.sigmoid(ga_ref[...].astype(F32))
    gc = jax.nn.sigmoid(gc_ref[...].astype(F32))
    o_ref[...] = (ga * a + gc * c).astype(o_ref.dtype)


def _merge(attn, conv, woa, woc, proj, *, off_ga, off_gc, tm, tn):
    t, qw = attn.shape
    cw = conv.shape[1]
    d = woa.shape[1]
    ga_blk, gc_blk = off_ga // tn, off_gc // tn
    return pl.pallas_call(
        _merge_kernel,
        grid=(t // tm, d // tn),
        in_specs=[pl.BlockSpec((tm, qw), lambda i, j: (i, 0)),
                  pl.BlockSpec((tm, cw), lambda i, j: (i, 0)),
                  pl.BlockSpec((qw, tn), lambda i, j: (0, j)),
                  pl.BlockSpec((cw, tn), lambda i, j: (0, j)),
                  pl.BlockSpec((tm, tn), lambda i, j: (i, ga_blk + j)),
                  pl.BlockSpec((tm, tn), lambda i, j: (i, gc_blk + j))],
        out_specs=pl.BlockSpec((tm, tn), lambda i, j: (i, j)),
        out_shape=jax.ShapeDtypeStruct((t, d), BF16),
        compiler_params=_params(("parallel", "arbitrary")),
        name="merge_proj",
    )(attn, conv, woa, woc, proj, proj)


def _head_rmsnorm(x, g):
    ms = jnp.mean(x * x, axis=-1, keepdims=True)
    return x * lax.rsqrt(ms + EPS) * g


def _attn_kernel(sink_ref, q_ref, kp_ref, kc_ref, kn_ref, vp_ref, vc_ref, vn_ref,
                 qg_ref, kg_ref, o_ref, *, nb, n_kv, n_heads):
    n = pl.program_id(0) % nb
    has_prev = n > 0
    has_next = n < nb - 1
    qi = lax.broadcasted_iota(jnp.int32, (BLOCK, 3 * BLOCK), 0)
    kj = lax.broadcasted_iota(jnp.int32, (BLOCK, 3 * BLOCK), 1)
    dist = jnp.abs(kj - BLOCK - qi)
    valid = ((dist <= WINDOW)
             & ((kj >= BLOCK) | has_prev)
             & ((kj < 2 * BLOCK) | has_next))
    distf = dist.astype(F32)
    qg = qg_ref[...]
    kg = kg_ref[...]
    scale = HEAD_DIM ** -0.5
    heads = range(n_heads)
    cols = [slice(h * HEAD_DIM, (h + 1) * HEAD_DIM) for h in heads]
    kbs = [_head_rmsnorm(jnp.concatenate([kp_ref[:, cols[kv]], kc_ref[:, cols[kv]],
                                          kn_ref[:, cols[kv]]], axis=0).astype(F32),
                         kg).astype(BF16) for kv in range(n_kv)]
    vs = [jnp.concatenate([vp_ref[:, cols[kv]], vc_ref[:, cols[kv]], vn_ref[:, cols[kv]]],
                          axis=0) for kv in range(n_kv)]
    qbs = [_head_rmsnorm(q_ref[:, cols[h]].astype(F32), qg).astype(BF16) for h in heads]
    ss = [lax.dot_general(qbs[h], kbs[h // GQA_GROUP], (((1,), (1,)), ((), ())),
                          preferred_element_type=F32) for h in heads]
    ss = [jnp.where(valid, ss[h] * scale - 2.0 ** (-8.0 * (h + 1) / n_heads) * distf, -jnp.inf)
          for h in heads]
    sinks = [sink_ref[h] for h in heads]
    ms = [jnp.maximum(jnp.max(ss[h], axis=-1, keepdims=True), sinks[h]) for h in heads]
    ps = [jnp.exp(ss[h] - ms[h]) for h in heads]
    invs = [1.0 / (jnp.sum(ps[h], axis=-1, keepdims=True) + jnp.exp(sinks[h] - ms[h]))
            for h in heads]
    for h in heads:
        probs = (ps[h] * invs[h]).astype(BF16)
        o = jnp.dot(probs, vs[h // GQA_GROUP], preferred_element_type=F32)
        o_ref[:, cols[h]] = o.astype(o_ref.dtype)


def _attention(proj, sink, qg, kg, *, seq, q_w, kv_w):
    t = proj.shape[0]
    nb = seq // BLOCK
    n_heads = q_w // HEAD_DIM
    n_kv = kv_w // HEAD_DIM
    k_blk = q_w // kv_w
    v_blk = k_blk + 1

    def prev_row(r):
        return jnp.where(r % nb > 0, r - 1, r)

    def next_row(r):
        return jnp.where(r % nb < nb - 1, r + 1, r)

    kv_specs = [pl.BlockSpec((BLOCK, kv_w), lambda r, c=c, f=f: (f(r), c))
                for c in (k_blk, v_blk) for f in (prev_row, lambda r: r, next_row)]
    return pl.pallas_call(
        functools.partial(_attn_kernel, nb=nb, n_kv=n_kv, n_heads=n_heads),
        grid=(t // BLOCK,),
        in_specs=[pl.BlockSpec(memory_space=pltpu.SMEM),
                  pl.BlockSpec((BLOCK, q_w), lambda r: (r, 0))]
                 + kv_specs
                 + [pl.BlockSpec((1, HEAD_DIM), lambda r: (0, 0)),
                    pl.BlockSpec((1, HEAD_DIM), lambda r: (0, 0))],
        out_specs=pl.BlockSpec((BLOCK, q_w), lambda r: (r, 0)),
        out_shape=jax.ShapeDtypeStruct((t, q_w), BF16),
        compiler_params=_params(("parallel",)),
        name="banded_attention",
    )(sink, proj, proj, proj, proj, proj, proj, proj,
      qg.reshape(1, HEAD_DIM), kg.reshape(1, HEAD_DIM))


def _conv_kernel(h_ref, b_ref, c_ref, hp_ref, cp_ref, hn_ref, cn_ref, w_ref, o_ref,
                 *, tiles_per_seq):
    i = pl.program_id(0)
    tm = h_ref.shape[0]
    u = c_ref[...].astype(F32) * h_ref[...].astype(F32)
    first = (i % tiles_per_seq) == 0
    last = (i % tiles_per_seq) == tiles_per_seq - 1
    u_prev = (cp_ref[SUBLANES - 1:SUBLANES, :].astype(F32)
              * hp_ref[SUBLANES - 1:SUBLANES, :].astype(F32))
    u_next = cn_ref[0:1, :].astype(F32) * hn_ref[0:1, :].astype(F32)
    u_prev = jnp.where(first, 0.0, u_prev)
    u_next = jnp.where(last, 0.0, u_next)
    row = lax.broadcasted_iota(jnp.int32, u.shape, 0)
    u_m1 = jnp.where(row == 0, u_prev, pltpu.roll(u, 1, axis=0))
    u_p1 = jnp.where(row == tm - 1, u_next, pltpu.roll(u, tm - 1, axis=0))
    w = w_ref[...]
    y = w[0:1, :] * u_m1 + w[1:2, :] * u + w[2:3, :] * u_p1
    o_ref[...] = (b_ref[...].astype(F32) * y).astype(o_ref.dtype)


def _short_conv(proj, conv_w, *, seq, off_h, conv_width, tm):
    t = proj.shape[0]
    cw = conv_width // 2
    nc = conv_width // cw
    hb, bb, cb = off_h // cw, (off_h + conv_width) // cw, (off_h + 2 * conv_width) // cw
    rows8 = tm // SUBLANES
    last8 = t // SUBLANES - 1

    def prev8(i):
        return jnp.maximum(i * rows8 - 1, 0)

    def next8(i):
        return jnp.minimum((i + 1) * rows8, last8)

    w_pad = jnp.zeros((SUBLANES, conv_width), F32).at[:CONV_K].set(conv_w)
    return pl.pallas_call(
        functools.partial(_conv_kernel, tiles_per_seq=seq // tm),
        grid=(t // tm, nc),
        in_specs=[pl.BlockSpec((tm, cw), lambda i, j: (i, hb + j)),
                  pl.BlockSpec((tm, cw), lambda i, j: (i, bb + j)),
                  pl.BlockSpec((tm, cw), lambda i, j: (i, cb + j)),
                  pl.BlockSpec((SUBLANES, cw), lambda i, j: (prev8(i), hb + j)),
                  pl.BlockSpec((SUBLANES, cw), lambda i, j: (prev8(i), cb + j)),
                  pl.BlockSpec((SUBLANES, cw), lambda i, j: (next8(i), hb + j)),
                  pl.BlockSpec((SUBLANES, cw), lambda i, j: (next8(i), cb + j)),
                  pl.BlockSpec((SUBLANES, cw), lambda i, j: (0, j))],
        out_specs=pl.BlockSpec((tm, cw), lambda i, j: (i, j)),
        out_shape=jax.ShapeDtypeStruct((t, conv_width), BF16),
        compiler_params=_params(("parallel", "arbitrary")),
        name="short_conv",
    )(proj, proj, proj, proj, proj, proj, proj, w_pad)


def _extract_top(work, count, *, break_ties):
    n = work.shape[0]
    iota = lax.broadcasted_iota(jnp.int32, work.shape, 0).astype(F32)
    rank = jnp.full(work.shape, float(count), F32)
    vals = []
    for r in range(count):
        m = jnp.max(work, axis=0, keepdims=True)
        vals.append(m)
        hit = work == m
        if break_ties:
            first = jnp.min(jnp.where(hit, iota, float(n)), axis=0, keepdims=True)
            hit = iota == first
        work = jnp.where(hit, -jnp.inf, work)
        rank = jnp.where(hit, float(r), rank)
    return vals, rank


def _rank_and_gate(s1, s2, *, break_ties):
    tm = s1.shape[1]
    v12, rank12 = _extract_top(jnp.concatenate([s1, s2], axis=1), PEER_TOPK,
                               break_ties=break_ties)
    v1, v2 = [v[:, :tm] for v in v12], [v[:, tm:] for v in v12]
    rank1, rank2 = rank12[:, :tm], rank12[:, tm:]

    v2_all = jnp.concatenate(v2, axis=0)
    v2_low = v2_all[:SUBLANES]
    row = lax.broadcasted_iota(jnp.int32, (SUBLANES, tm), 0)
    pieces = [v1[0] + v2_all]
    for a in range(1, SUBLANES):
        nb = PEER_TOPK // (a + 1)
        piece = v1[a] + v2_low
        pieces.append(piece if nb >= SUBLANES else jnp.where(row < nb, piece, -jnp.inf))
    pieces.append(jnp.concatenate(v1[SUBLANES:], axis=0) + v2[0])
    cand = jnp.concatenate(pieces, axis=0)
    top, cand_rank = _extract_top(cand, PEER_TOPK, break_ties=break_ties)
    chosen = jnp.where(cand_rank < PEER_TOPK, 1.0, 0.0)

    cnt_a = [jnp.sum(chosen[:PEER_TOPK], axis=0, keepdims=True)]
    for a in range(1, SUBLANES):
        lo = PEER_TOPK + (a - 1) * SUBLANES
        cnt_a.append(jnp.sum(chosen[lo:lo + SUBLANES], axis=0, keepdims=True))
    lo = PEER_TOPK + (SUBLANES - 1) * SUBLANES
    cnt_a += [chosen[lo + k:lo + k + 1] for k in range(PEER_TOPK - SUBLANES)]
    cnt = jnp.zeros_like(s1)
    for a in range(PEER_TOPK):
        cnt = jnp.where(rank1 == float(a), cnt_a[a], cnt)

    m1, m2 = v1[0], v2[0]
    denom = jnp.exp(top[0] - (m1 + m2))
    for r in range(1, PEER_TOPK):
        denom = denom + jnp.exp(top[r] - (m1 + m2))
    e2 = jnp.exp(s2 - m2)
    c = jnp.exp(s1 - m1) / denom

    n_ranked = jnp.sum(jnp.where(rank12 < PEER_TOPK, 1.0, 0.0), axis=0, keepdims=True)
    n_chosen = sum(cnt_a[1:], cnt_a[0])
    most = jnp.maximum(jnp.max(n_ranked), jnp.max(n_chosen))
    return rank2, e2, cnt, c, most


def _peer_score_kernel(xt_ref, wq_ref, keys_ref, r2_ref, e2_ref, cnt_ref, c_ref):
    qt = jnp.dot(wq_ref[...], xt_ref[...], preferred_element_type=F32)
    q1 = qt[:PEER_KEY_DIM].astype(BF16)
    q2 = qt[PEER_KEY_DIM:].astype(BF16)
    s1 = jnp.dot(keys_ref[0, 0], q1, preferred_element_type=F32)
    s2 = jnp.dot(keys_ref[0, 1], q2, preferred_element_type=F32)
    tm = s1.shape[1]

    def write(rank2, e2, cnt, c):
        r2_ref[...] = rank2.astype(r2_ref.dtype)
        e2_ref[...] = e2.astype(e2_ref.dtype)
        cnt_ref[:, 0] = cnt.reshape(N_KEYS // SUBLANES, SUBLANES, tm)
        c_ref[:, 0] = c.reshape(N_KEYS // SUBLANES, SUBLANES, tm)

    *result, most = _rank_and_gate(s1, s2, break_ties=False)
    write(*result)

    @pl.when(most > PEER_TOPK)
    def _():
        write(*_rank_and_gate(s1, s2, break_ties=True)[:4])


def _peer_score(xt, wq_t, keys, *, tm):
    d, t = xt.shape
    qd = 2 * PEER_KEY_DIM
    groups = N_KEYS // SUBLANES
    dense = jax.ShapeDtypeStruct((PEER_HEADS * N_KEYS, t), BF16)
    grouped = jax.ShapeDtypeStruct((groups, PEER_HEADS, SUBLANES, t), F32)
    dense_spec = pl.BlockSpec((N_KEYS, tm), lambda i, h: (h, i))
    grouped_spec = pl.BlockSpec((groups, 1, SUBLANES, tm), lambda i, h: (0, h, 0, i))
    return pl.pallas_call(
        _peer_score_kernel,
        grid=(t // tm, PEER_HEADS),
        in_specs=[pl.BlockSpec((d, tm), lambda i, h: (0, i)),
                  pl.BlockSpec((qd, d), lambda i, h: (h, 0)),
                  pl.BlockSpec((1, 2, N_KEYS, PEER_KEY_DIM), lambda i, h: (h, 0, 0, 0))],
        out_specs=[dense_spec, dense_spec, grouped_spec, grouped_spec],
        out_shape=[dense, dense, grouped, grouped],
        compiler_params=_params(("parallel", "arbitrary")),
        name="peer_score",
    )(xt, wq_t, keys)


def _gelu(x):
    return 0.5 * x * (1.0 + lax.erf(x * (2.0 ** -0.5)))


def _build_gate_weights(r2_ref, e2_ref, cnt_ref, c_ref, w_ref, sub, n_i):
    tm = w_ref.shape[1]
    groups_per_tile = SUBLANES // n_i
    chunk_rows = SUBLANES * MIX_ACC_VREGS // n_i
    for tl in range(tm // BF16_COLS):
        cols = slice(tl * BF16_COLS, (tl + 1) * BF16_COLS)
        for jc in range(N_KEYS // chunk_rows):
            rows = slice(jc * chunk_rows, (jc + 1) * chunk_rows)
            acc = [None] * n_i
            packed = (chunk_rows // SUBLANES, SUBLANES, BF16_COLS)
            for h in range(PEER_HEADS):
                hrows = slice(h * N_KEYS + rows.start, h * N_KEYS + rows.stop)
                r2c = r2_ref[hrows, cols].reshape(packed)
                e2c = e2_ref[hrows, cols].reshape(packed)
                for ii in range(n_i):
                    cnt_row = cnt_ref[0, h, ii:ii + 1, cols]
                    c_row = c_ref[0, h, ii:ii + 1, cols]
                    for alt in range(1, groups_per_tile):
                        r = alt * n_i + ii
                        cnt_row = jnp.where(sub == alt, cnt_ref[0, h, r:r + 1, cols], cnt_row)
                        c_row = jnp.where(sub == alt, c_ref[0, h, r:r + 1, cols], c_row)
                    cnt_b = jnp.broadcast_to(cnt_row, (SUBLANES, BF16_COLS)).astype(BF16)[None]
                    c_b = jnp.broadcast_to(c_row, (SUBLANES, BF16_COLS)).astype(BF16)[None]
                    term = jnp.where(r2c < cnt_b, e2c * c_b, jnp.zeros((), BF16))
                    acc[ii] = term if acc[ii] is None else acc[ii] + term
            for ii in range(n_i):
                r0 = ii * N_KEYS + jc * chunk_rows
                w_ref[r0:r0 + chunk_rows, cols] = acc[ii].reshape(chunk_rows, BF16_COLS)


def _peer_mix_kernel(xt_ref, wd_ref, wu_ref, r2_ref, e2_ref, cnt_ref, c_ref, o_ref, w_ref,
                     *, n_i):
    e = pl.program_id(1)

    @pl.when(e == 0)
    def _():
        o_ref[...] = jnp.zeros_like(o_ref)

    _build_gate_weights(r2_ref, e2_ref, cnt_ref, c_ref, w_ref, e % (SUBLANES // n_i), n_i)
    a = jnp.dot(wd_ref[...], xt_ref[...], preferred_element_type=F32)
    gated = _gelu(a).astype(BF16) * w_ref[...]
    o_ref[...] += jnp.dot(wu_ref[...], gated, preferred_element_type=F32)


def _peer_mix(xt, w_down, w_up_t, rank2, e2, cnt, c, *, tm, eb):
    d, t = xt.shape
    ne = w_down.shape[0]
    n_i = eb // N_KEYS
    groups_per_tile = SUBLANES // n_i
    once = pl.Buffered(1)
    dense_spec = pl.BlockSpec((PEER_HEADS * N_KEYS, tm), lambda i, e: (0, i),
                              pipeline_mode=once)
    grouped_spec = pl.BlockSpec((1, PEER_HEADS, SUBLANES, tm),
                                lambda i, e: (e // groups_per_tile, 0, 0, i))
    return pl.pallas_call(
        functools.partial(_peer_mix_kernel, n_i=n_i),
        grid=(t // tm, ne // eb),
        in_specs=[pl.BlockSpec((d, tm), lambda i, e: (0, i), pipeline_mode=once),
                  pl.BlockSpec((eb, d), lambda i, e: (e, 0)),
                  pl.BlockSpec((d, eb), lambda i, e: (0, e)),
                  dense_spec, dense_spec, grouped_spec, grouped_spec],
        out_specs=pl.BlockSpec((d, tm), lambda i, e: (0, i), pipeline_mode=once),
        out_shape=jax.ShapeDtypeStruct((d, t), F32),
        scratch_shapes=[pltpu.VMEM((eb, tm), BF16)],
        compiler_params=_params(("parallel", "arbitrary"), PEER_MIX_VMEM_LIMIT),
        name="peer_mix",
    )(xt, w_down, w_up_t, rank2, e2, cnt, c)


def _add_t_kernel(x_ref, pt_ref, o_ref):
    o_ref[...] = x_ref[...] + pt_ref[...].T


def _add_transposed(x, pt, *, tm):
    t, d = x.shape
    return pl.pallas_call(
        _add_t_kernel,
        grid=(t // tm,),
        in_specs=[pl.BlockSpec((tm, d), lambda i: (i, 0)),
                  pl.BlockSpec((d, tm), lambda i: (0, i))],
        out_specs=pl.BlockSpec((tm, d), lambda i: (i, 0)),
        out_shape=jax.ShapeDtypeStruct((t, d), F32),
        compiler_params=_params(("parallel",)),
        name="add_transposed",
    )(x, pt)


def _tiles(t, d, in_w):
    def pick(n, cap):
        b = cap
        while n % b:
            b //= 2
        return b
    return dict(
        norm_tm=pick(t, 256),
        mm_tm=pick(t, 1024),
        mm_tn=pick(d // 4, 1024),
        conv_tm=pick(t, 512),
        peer_tm=pick(t, 512),
        score_tm=pick(t, 1024),
        peer_eb=1024,
        cast_tr=256,
    )


def _layer(x2, seq, norm1_g, w_in, q_norm_g, k_norm_g, sink_logits, conv_w, w_o_attn,
           w_o_conv, w_out, norm2_g, w_q_peer, sub_keys, w_down, w_up):
    t, d = x2.shape
    in_w = w_in.shape[1]
    q_w = w_o_attn.shape[0]
    kv_w = q_w // GQA_GROUP
    conv_width = w_o_conv.shape[0]
    off_h = q_w + 2 * kv_w
    off_ga = off_h + 3 * conv_width
    off_gc = off_ga + d
    tl = _tiles(t, d, in_w)

    xn = _rmsnorm(x2, norm1_g, tm=tl["norm_tm"])
    proj = _matmul(xn, w_in.astype(BF16), tm=tl["mm_tm"], tn=tl["mm_tn"],
                   out_dtype=BF16, name="in_proj")
    attn = _attention(proj, sink_logits.astype(F32), q_norm_g, k_norm_g,
                      seq=seq, q_w=q_w, kv_w=kv_w)
    conv = _short_conv(proj, conv_w, seq=seq, off_h=off_h, conv_width=conv_width,
                       tm=tl["conv_tm"])
    merged = _merge(attn, conv, w_o_attn.astype(BF16), w_o_conv.astype(BF16), proj,
                    off_ga=off_ga, off_gc=off_gc, tm=tl["mm_tm"], tn=tl["mm_tn"])
    x1 = _matmul_residual(x2, merged, w_out.astype(BF16), tm=tl["mm_tm"], tn=tl["mm_tn"])

    xn2_t = _rmsnorm(x1, norm2_g, tm=tl["norm_tm"], transpose=True)
    wq_t = _cast_transpose(w_q_peer, tr=tl["cast_tr"])
    rank2, e2, cnt, c = _peer_score(xn2_t, wq_t, sub_keys.astype(BF16),
                                    tm=tl["score_tm"])
    w_up_t = _cast_transpose(w_up, tr=tl["cast_tr"])
    peer_t = _peer_mix(xn2_t, w_down.astype(BF16), w_up_t, rank2, e2, cnt, c,
                       tm=tl["peer_tm"], eb=tl["peer_eb"])
    return _add_transposed(x1, peer_t, tm=tl["norm_tm"])


def kernel(x, norm1_g, w_in, q_norm_g, k_norm_g, sink_logits, conv_w, w_o_attn, w_o_conv,
           w_out, norm2_g, w_q_peer, sub_keys, w_down, w_up):
    b, s, d = x.shape
    x2 = x.reshape(b * s, d)
    for i in range(norm1_g.shape[0]):
        x2 = _layer(x2, s, norm1_g[i], w_in[i], q_norm_g[i], k_norm_g[i], sink_logits[i],
                    conv_w[i], w_o_attn[i], w_o_conv[i], w_out[i], norm2_g[i],
                    w_q_peer[i], sub_keys[i], w_down[i], w_up[i])
    return x2.reshape(b, s, d)
```

```python
import functools

import jax
import jax.numpy as jnp
from jax import lax
from jax.experimental import pallas as pl
from jax.experimental.pallas import tpu as pltpu

F32 = jnp.float32
BF16 = jnp.bfloat16

HEAD_DIM = 128
GQA_GROUP = 4
WINDOW = 128
BLOCK = 128
CONV_K = 3
PEER_HEADS = 8
N_KEYS = 128
PEER_KEY_DIM = 128
PEER_TOPK = 16
EPS = 1e-6
SUBLANES = 8
LANES = 128
BF16_COLS = 2 * LANES
MIX_ACC_VREGS = 32
VMEM_LIMIT = 56 * 1024 * 1024
PEER_MIX_VMEM_LIMIT = 60 * 1024 * 1024


def _params(sem, vmem_limit=VMEM_LIMIT):
    return pltpu.CompilerParams(dimension_semantics=sem, vmem_limit_bytes=vmem_limit)


def _rmsnorm_kernel(x_ref, g_ref, o_ref, *, transpose):
    x = x_ref[...]
    ms = jnp.mean(x * x, axis=-1, keepdims=True)
    y = x * lax.rsqrt(ms + EPS) * g_ref[...]
    if transpose:
        y = y.T
    o_ref[...] = y.astype(o_ref.dtype)


def _rmsnorm(x, g, *, tm, transpose=False):
    t, d = x.shape
    if transpose:
        out_shape = jax.ShapeDtypeStruct((d, t), BF16)
        out_spec = pl.BlockSpec((d, tm), lambda i: (0, i))
    else:
        out_shape = jax.ShapeDtypeStruct((t, d), BF16)
        out_spec = pl.BlockSpec((tm, d), lambda i: (i, 0))
    return pl.pallas_call(
        functools.partial(_rmsnorm_kernel, transpose=transpose),
        grid=(t // tm,),
        in_specs=[pl.BlockSpec((tm, d), lambda i: (i, 0)),
                  pl.BlockSpec((1, d), lambda i: (0, 0))],
        out_specs=out_spec,
        out_shape=out_shape,
        compiler_params=_params(("parallel",)),
        name="rmsnorm_t" if transpose else "rmsnorm",
    )(x, g.reshape(1, d))


def _cast_t_kernel(x_ref, o_ref):
    o_ref[...] = x_ref[...].T.astype(o_ref.dtype)


def _cast_transpose(w, *, tr):
    r, c = w.shape
    return pl.pallas_call(
        _cast_t_kernel,
        grid=(r // tr,),
        in_specs=[pl.BlockSpec((tr, c), lambda i: (i, 0))],
        out_specs=pl.BlockSpec((c, tr), lambda i: (0, i)),
        out_shape=jax.ShapeDtypeStruct((c, r), BF16),
        compiler_params=_params(("parallel",)),
        name="cast_transpose",
    )(w)


def _mm_kernel(a_ref, b_ref, o_ref):
    o_ref[...] = jnp.dot(a_ref[...], b_ref[...],
                         preferred_element_type=F32).astype(o_ref.dtype)


def _matmul(a, b, *, tm, tn, out_dtype, name):
    m, k = a.shape
    n = b.shape[1]
    return pl.pallas_call(
        _mm_kernel,
        grid=(m // tm, n // tn),
        in_specs=[pl.BlockSpec((tm, k), lambda i, j: (i, 0)),
                  pl.BlockSpec((k, tn), lambda i, j: (0, j))],
        out_specs=pl.BlockSpec((tm, tn), lambda i, j: (i, j)),
        out_shape=jax.ShapeDtypeStruct((m, n), out_dtype),
        compiler_params=_params(("parallel", "arbitrary")),
        name=name,
    )(a, b)


def _mm_res_kernel(r_ref, a_ref, b_ref, o_ref):
    o_ref[...] = r_ref[...] + jnp.dot(a_ref[...], b_ref[...], preferred_element_type=F32)


def _matmul_residual(r, a, b, *, tm, tn):
    m, k = a.shape
    n = b.shape[1]
    return pl.pallas_call(
        _mm_res_kernel,
        grid=(m // tm, n // tn),
        in_specs=[pl.BlockSpec((tm, tn), lambda i, j: (i, j)),
                  pl.BlockSpec((tm, k), lambda i, j: (i, 0)),
                  pl.BlockSpec((k, tn), lambda i, j: (0, j))],
        out_specs=pl.BlockSpec((tm, tn), lambda i, j: (i, j)),
        out_shape=jax.ShapeDtypeStruct((m, n), F32),
        compiler_params=_params(("parallel", "arbitrary")),
        name="out_proj_residual",
    )(r, a, b)


def _merge_kernel(attn_ref, conv_ref, woa_ref, woc_ref, ga_ref, gc_ref, o_ref):
    a = jnp.dot(attn_ref[...], woa_ref[...], preferred_element_type=F32)
    c = jnp.dot(conv_ref[...], woc_ref[...], preferred_element_type=F32)
    ga = jax.nn.sigmoid(ga_ref[...].astype(F32))
    gc = jax.nn.sigmoid(gc_ref[...].astype(F32))
    o_ref[...] = (ga * a + gc * c).astype(o_ref.dtype)


def _merge(attn, conv, woa, woc, proj, *, off_ga, off_gc, tm, tn):
    t, qw = attn.shape
    cw = conv.shape[1]
    d = woa.shape[1]
    ga_blk, gc_blk = off_ga // tn, off_gc // tn
    return pl.pallas_call(
        _merge_kernel,
        grid=(t // tm, d // tn),
        in_specs=[pl.BlockSpec((tm, qw), lambda i, j: (i, 0)),
                  pl.BlockSpec((tm, cw), lambda i, j: (i, 0)),
                  pl.BlockSpec((qw, tn), lambda i, j: (0, j)),
                  pl.BlockSpec((cw, tn), lambda i, j: (0, j)),
                  pl.BlockSpec((tm, tn), lambda i, j: (i, ga_blk + j)),
                  pl.BlockSpec((tm, tn), lambda i, j: (i, gc_blk + j))],
        out_specs=pl.BlockSpec((tm, tn), lambda i, j: (i, j)),
        out_shape=jax.ShapeDtypeStruct((t, d), BF16),
        compiler_params=_params(("parallel", "arbitrary")),
        name="merge_proj",
    )(attn, conv, woa, woc, proj, proj)


def _head_rmsnorm(x, g):
    ms = jnp.mean(x * x, axis=-1, keepdims=True)
    return x * lax.rsqrt(ms + EPS) * g


def _attn_kernel(sink_ref, q_ref, kp_ref, kc_ref, kn_ref, vp_ref, vc_ref, vn_ref,
                 qg_ref, kg_ref, o_ref, *, nb, n_kv, n_heads):
    n = pl.program_id(0) % nb
    has_prev = n > 0
    has_next = n < nb - 1
    qi = lax.broadcasted_iota(jnp.int32, (BLOCK, 3 * BLOCK), 0)
    kj = lax.broadcasted_iota(jnp.int32, (BLOCK, 3 * BLOCK), 1)
    dist = jnp.abs(kj - BLOCK - qi)
    valid = ((dist <= WINDOW)
             & ((kj >= BLOCK) | has_prev)
             & ((kj < 2 * BLOCK) | has_next))
    distf = dist.astype(F32)
    qg = qg_ref[...]
    kg = kg_ref[...]
    scale = HEAD_DIM ** -0.5
    heads = range(n_heads)
    cols = [slice(h * HEAD_DIM, (h + 1) * HEAD_DIM) for h in heads]
    kbs = [_head_rmsnorm(jnp.concatenate([kp_ref[:, cols[kv]], kc_ref[:, cols[kv]],
                                          kn_ref[:, cols[kv]]], axis=0).astype(F32),
                         kg).astype(BF16) for kv in range(n_kv)]
    vs = [jnp.concatenate([vp_ref[:, cols[kv]], vc_ref[:, cols[kv]], vn_ref[:, cols[kv]]],
                          axis=0) for kv in range(n_kv)]
    qbs = [_head_rmsnorm(q_ref[:, cols[h]].astype(F32), qg).astype(BF16) for h in heads]
    ss = [lax.dot_general(qbs[h], kbs[h // GQA_GROUP], (((1,), (1,)), ((), ())),
                          preferred_element_type=F32) for h in heads]
    ss = [jnp.where(valid, ss[h] * scale - 2.0 ** (-8.0 * (h + 1) / n_heads) * distf, -jnp.inf)
          for h in heads]
    sinks = [sink_ref[h] for h in heads]
    ms = [jnp.maximum(jnp.max(ss[h], axis=-1, keepdims=True), sinks[h]) for h in heads]
    ps = [jnp.exp(ss[h] - ms[h]) for h in heads]
    invs = [1.0 / (jnp.sum(ps[h], axis=-1, keepdims=True) + jnp.exp(sinks[h] - ms[h]))
            for h in heads]
    for h in heads:
        probs = (ps[h] * invs[h]).astype(BF16)
        o = jnp.dot(probs, vs[h // GQA_GROUP], preferred_element_type=F32)
        o_ref[:, cols[h]] = o.astype(o_ref.dtype)


def _attention(proj, sink, qg, kg, *, seq, q_w, kv_w):
    t = proj.shape[0]
    nb = seq // BLOCK
    n_heads = q_w // HEAD_DIM
    n_kv = kv_w // HEAD_DIM
    k_blk = q_w // kv_w
    v_blk = k_blk + 1

    def prev_row(r):
        return jnp.where(r % nb > 0, r - 1, r)

    def next_row(r):
        return jnp.where(r % nb < nb - 1, r + 1, r)

    kv_specs = [pl.BlockSpec((BLOCK, kv_w), lambda r, c=c, f=f: (f(r), c))
                for c in (k_blk, v_blk) for f in (prev_row, lambda r: r, next_row)]
    return pl.pallas_call(
        functools.partial(_attn_kernel, nb=nb, n_kv=n_kv, n_heads=n_heads),
        grid=(t // BLOCK,),
        in_specs=[pl.BlockSpec(memory_space=pltpu.SMEM),
                  pl.BlockSpec((BLOCK, q_w), lambda r: (r, 0))]
                 + kv_specs
                 + [pl.BlockSpec((1, HEAD_DIM), lambda r: (0, 0)),
                    pl.BlockSpec((1, HEAD_DIM), lambda r: (0, 0))],
        out_specs=pl.BlockSpec((BLOCK, q_w), lambda r: (r, 0)),
        out_shape=jax.ShapeDtypeStruct((t, q_w), BF16),
        compiler_params=_params(("parallel",)),
        name="banded_attention",
    )(sink, proj, proj, proj, proj, proj, proj, proj,
      qg.reshape(1, HEAD_DIM), kg.reshape(1, HEAD_DIM))


def _conv_kernel(h_ref, b_ref, c_ref, hp_ref, cp_ref, hn_ref, cn_ref, w_ref, o_ref,
                 *, tiles_per_seq):
    i = pl.program_id(0)
    tm = h_ref.shape[0]
    u = c_ref[...].astype(F32) * h_ref[...].astype(F32)
    first = (i % tiles_per_seq) == 0
    last = (i % tiles_per_seq) == tiles_per_seq - 1
    u_prev = (cp_ref[SUBLANES - 1:SUBLANES, :].astype(F32)
              * hp_ref[SUBLANES - 1:SUBLANES, :].astype(F32))
    u_next = cn_ref[0:1, :].astype(F32) * hn_ref[0:1, :].astype(F32)
    u_prev = jnp.where(first, 0.0, u_prev)
    u_next = jnp.where(last, 0.0, u_next)
    row = lax.broadcasted_iota(jnp.int32, u.shape, 0)
    u_m1 = jnp.where(row == 0, u_prev, pltpu.roll(u, 1, axis=0))
    u_p1 = jnp.where(row == tm - 1, u_next, pltpu.roll(u, tm - 1, axis=0))
    w = w_ref[...]
    y = w[0:1, :] * u_m1 + w[1:2, :] * u + w[2:3, :] * u_p1
    o_ref[...] = (b_ref[...].astype(F32) * y).astype(o_ref.dtype)


def _short_conv(proj, conv_w, *, seq, off_h, conv_width, tm):
    t = proj.shape[0]
    cw = conv_width // 2
    nc = conv_width // cw
    hb, bb, cb = off_h // cw, (off_h + conv_width) // cw, (off_h + 2 * conv_width) // cw
    rows8 = tm // SUBLANES
    last8 = t // SUBLANES - 1

    def prev8(i):
        return jnp.maximum(i * rows8 - 1, 0)

    def next8(i):
        return jnp.minimum((i + 1) * rows8, last8)

    w_pad = jnp.zeros((SUBLANES, conv_width), F32).at[:CONV_K].set(conv_w)
    return pl.pallas_call(
        functools.partial(_conv_kernel, tiles_per_seq=seq // tm),
        grid=(t // tm, nc),
        in_specs=[pl.BlockSpec((tm, cw), lambda i, j: (i, hb + j)),
                  pl.BlockSpec((tm, cw), lambda i, j: (i, bb + j)),
                  pl.BlockSpec((tm, cw), lambda i, j: (i, cb + j)),
                  pl.BlockSpec((SUBLANES, cw), lambda i, j: (prev8(i), hb + j)),
                  pl.BlockSpec((SUBLANES, cw), lambda i, j: (prev8(i), cb + j)),
                  pl.BlockSpec((SUBLANES, cw), lambda i, j: (next8(i), hb + j)),
                  pl.BlockSpec((SUBLANES, cw), lambda i, j: (next8(i), cb + j)),
                  pl.BlockSpec((SUBLANES, cw), lambda i, j: (0, j))],
        out_specs=pl.BlockSpec((tm, cw), lambda i, j: (i, j)),
        out_shape=jax.ShapeDtypeStruct((t, conv_width), BF16),
        compiler_params=_params(("parallel", "arbitrary")),
        name="short_conv",
    )(proj, proj, proj, proj, proj, proj, proj, w_pad)


def _extract_top(work, count, *, break_ties):
    n = work.shape[0]
    iota = lax.broadcasted_iota(jnp.int32, work.shape, 0).astype(F32)
    rank = jnp.full(work.shape, float(count), F32)
    vals = []
    for r in range(count):
        m = jnp.max(work, axis=0, keepdims=True)
        vals.append(m)
        hit = work == m
        if break_ties:
            first = jnp.min(jnp.where(hit, iota, float(n)), axis=0, keepdims=True)
            hit = iota == first
        work = jnp.where(hit, -jnp.inf, work)
        rank = jnp.where(hit, float(r), rank)
    return vals, rank


def _rank_and_gate(s1, s2, *, break_ties):
    tm = s1.shape[1]
    v12, rank12 = _extract_top(jnp.concatenate([s1, s2], axis=1), PEER_TOPK,
                               break_ties=break_ties)
    v1, v2 = [v[:, :tm] for v in v12], [v[:, tm:] for v in v12]
    rank1, rank2 = rank12[:, :tm], rank12[:, tm:]

    v2_all = jnp.concatenate(v2, axis=0)
    v2_low = v2_all[:SUBLANES]
    row = lax.broadcasted_iota(jnp.int32, (SUBLANES, tm), 0)
    pieces = [v1[0] + v2_all]
    for a in range(1, SUBLANES):
        nb = PEER_TOPK // (a + 1)
        piece = v1[a] + v2_low
        pieces.append(piece if nb >= SUBLANES else jnp.where(row < nb, piece, -jnp.inf))
    pieces.append(jnp.concatenate(v1[SUBLANES:], axis=0) + v2[0])
    cand = jnp.concatenate(pieces, axis=0)
    top, cand_rank = _extract_top(cand, PEER_TOPK, break_ties=break_ties)
    chosen = jnp.where(cand_rank < PEER_TOPK, 1.0, 0.0)

    cnt_a = [jnp.sum(chosen[:PEER_TOPK], axis=0, keepdims=True)]
    for a in range(1, SUBLANES):
        lo = PEER_TOPK + (a - 1) * SUBLANES
        cnt_a.append(jnp.sum(chosen[lo:lo + SUBLANES], axis=0, keepdims=True))
    lo = PEER_TOPK + (SUBLANES - 1) * SUBLANES
    cnt_a += [chosen[lo + k:lo + k + 1] for k in range(PEER_TOPK - SUBLANES)]
    cnt = jnp.zeros_like(s1)
    for a in range(PEER_TOPK):
        cnt = jnp.where(rank1 == float(a), cnt_a[a], cnt)

    m1, m2 = v1[0], v2[0]
    denom = jnp.exp(top[0] - (m1 + m2))
    for r in range(1, PEER_TOPK):
        denom = denom + jnp.exp(top[r] - (m1 + m2))
    e2 = jnp.exp(s2 - m2)
    c = jnp.exp(s1 - m1) / denom

    n_ranked = jnp.sum(jnp.where(rank12 < PEER_TOPK, 1.0, 0.0), axis=0, keepdims=True)
    n_chosen = sum(cnt_a[1:], cnt_a[0])
    most = jnp.maximum(jnp.max(n_ranked), jnp.max(n_chosen))
    return rank2, e2, cnt, c, most


def _peer_score_kernel(xt_ref, wq_ref, keys_ref, r2_ref, e2_ref, cnt_ref, c_ref):
    qt = jnp.dot(wq_ref[...], xt_ref[...], preferred_element_type=F32)
    q1 = qt[:PEER_KEY_DIM].astype(BF16)
    q2 = qt[PEER_KEY_DIM:].astype(BF16)
    s1 = jnp.dot(keys_ref[0, 0], q1, preferred_element_type=F32)
    s2 = jnp.dot(keys_ref[0, 1], q2, preferred_element_type=F32)
    tm = s1.shape[1]

    def write(rank2, e2, cnt, c):
        r2_ref[...] = rank2.astype(r2_ref.dtype)
        e2_ref[...] = e2.astype(e2_ref.dtype)
        cnt_ref[:, 0] = cnt.reshape(N_KEYS // SUBLANES, SUBLANES, tm)
        c_ref[:, 0] = c.reshape(N_KEYS // SUBLANES, SUBLANES, tm)

    *result, most = _rank_and_gate(s1, s2, break_ties=False)
    write(*result)

    @pl.when(most > PEER_TOPK)
    def _():
        write(*_rank_and_gate(s1, s2, break_ties=True)[:4])


def _peer_score(xt, wq_t, keys, *, tm):
    d, t = xt.shape
    qd = 2 * PEER_KEY_DIM
    groups = N_KEYS // SUBLANES
    dense = jax.ShapeDtypeStruct((PEER_HEADS * N_KEYS, t), BF16)
    grouped = jax.ShapeDtypeStruct((groups, PEER_HEADS, SUBLANES, t), F32)
    dense_spec = pl.BlockSpec((N_KEYS, tm), lambda i, h: (h, i))
    grouped_spec = pl.BlockSpec((groups, 1, SUBLANES, tm), lambda i, h: (0, h, 0, i))
    return pl.pallas_call(
        _peer_score_kernel,
        grid=(t // tm, PEER_HEADS),
        in_specs=[pl.BlockSpec((d, tm), lambda i, h: (0, i)),
                  pl.BlockSpec((qd, d), lambda i, h: (h, 0)),
                  pl.BlockSpec((1, 2, N_KEYS, PEER_KEY_DIM), lambda i, h: (h, 0, 0, 0))],
        out_specs=[dense_spec, dense_spec, grouped_spec, grouped_spec],
        out_shape=[dense, dense, grouped, grouped],
        compiler_params=_params(("parallel", "arbitrary")),
        name="peer_score",
    )(xt, wq_t, keys)


def _gelu(x):
    return 0.5 * x * (1.0 + lax.erf(x * (2.0 ** -0.5)))


def _build_gate_weights(r2_ref, e2_ref, cnt_ref, c_ref, w_ref, rows_ref, sub, n_i):
    tm = w_ref.shape[1]
    groups_per_tile = SUBLANES // n_i
    chunk_rows = SUBLANES * MIX_ACC_VREGS // n_i
    for h in range(PEER_HEADS):
        for ii in range(n_i):
            cnt_row = cnt_ref[0, h, ii:ii + 1, :]
            c_row = c_ref[0, h, ii:ii + 1, :]
            for alt in range(1, groups_per_tile):
                r = alt * n_i + ii
                cnt_row = jnp.where(sub == alt, cnt_ref[0, h, r:r + 1, :], cnt_row)
                c_row = jnp.where(sub == alt, c_ref[0, h, r:r + 1, :], c_row)
            rows_ref[0, h * n_i + ii] = jnp.broadcast_to(cnt_row, (SUBLANES, tm)).astype(BF16)
            rows_ref[1, h * n_i + ii] = jnp.broadcast_to(c_row, (SUBLANES, tm)).astype(BF16)
    for tl in range(tm // BF16_COLS):
        cols = slice(tl * BF16_COLS, (tl + 1) * BF16_COLS)
        for jc in range(N_KEYS // chunk_rows):
            rows = slice(jc * chunk_rows, (jc + 1) * chunk_rows)
            acc = [None] * n_i
            packed = (chunk_rows // SUBLANES, SUBLANES, BF16_COLS)
            for h in range(PEER_HEADS):
                hrows = slice(h * N_KEYS + rows.start, h * N_KEYS + rows.stop)
                r2c = r2_ref[hrows, cols].reshape(packed)
                e2c = e2_ref[hrows, cols].reshape(packed)
                for ii in range(n_i):
                    cnt_b = rows_ref[0, h * n_i + ii, :, cols][None]
                    c_b = rows_ref[1, h * n_i + ii, :, cols][None]
                    term = jnp.where(r2c < cnt_b, e2c * c_b, jnp.zeros((), BF16))
                    acc[ii] = term if acc[ii] is None else acc[ii] + term
            for ii in range(n_i):
                r0 = ii * N_KEYS + jc * chunk_rows
                w_ref[r0:r0 + chunk_rows, cols] = acc[ii].reshape(chunk_rows, BF16_COLS)


def _peer_mix_kernel(xt_ref, wd_ref, wu_ref, r2_ref, e2_ref, cnt_ref, c_ref, o_ref, w_ref,
                     rows_ref, *, n_i):
    e = pl.program_id(1)

    @pl.when(e == 0)
    def _():
        o_ref[...] = jnp.zeros_like(o_ref)

    _build_gate_weights(r2_ref, e2_ref, cnt_ref, c_ref, w_ref, rows_ref,
                        e % (SUBLANES // n_i), n_i)
    a = jnp.dot(wd_ref[...], xt_ref[...], preferred_element_type=F32)
    gated = _gelu(a).astype(BF16) * w_ref[...]
    o_ref[...] += jnp.dot(wu_ref[...], gated, preferred_element_type=F32)


def _peer_mix(xt, w_down, w_up_t, rank2, e2, cnt, c, *, tm, eb):
    d, t = xt.shape
    ne = w_down.shape[0]
    n_i = eb // N_KEYS
    groups_per_tile = SUBLANES // n_i
    once = pl.Buffered(1)
    dense_spec = pl.BlockSpec((PEER_HEADS * N_KEYS, tm), lambda i, e: (0, i),
                              pipeline_mode=once)
    grouped_spec = pl.BlockSpec((1, PEER_HEADS, SUBLANES, tm),
                                lambda i, e: (e // groups_per_tile, 0, 0, i))
    return pl.pallas_call(
        functools.partial(_peer_mix_kernel, n_i=n_i),
        grid=(t // tm, ne // eb),
        in_specs=[pl.BlockSpec((d, tm), lambda i, e: (0, i), pipeline_mode=once),
                  pl.BlockSpec((eb, d), lambda i, e: (e, 0)),
                  pl.BlockSpec((d, eb), lambda i, e: (0, e)),
                  dense_spec, dense_spec, grouped_spec, grouped_spec],
        out_specs=pl.BlockSpec((d, tm), lambda i, e: (0, i), pipeline_mode=once),
        out_shape=jax.ShapeDtypeStruct((d, t), F32),
        scratch_shapes=[pltpu.VMEM((eb, tm), BF16),
                        pltpu.VMEM((2, PEER_HEADS * n_i, SUBLANES, tm), BF16)],
        compiler_params=_params(("parallel", "arbitrary"), PEER_MIX_VMEM_LIMIT),
        name="peer_mix",
    )(xt, w_down, w_up_t, rank2, e2, cnt, c)


def _add_t_kernel(x_ref, pt_ref, o_ref):
    o_ref[...] = x_ref[...] + pt_ref[...].T


def _add_transposed(x, pt, *, tm):
    t, d = x.shape
    return pl.pallas_call(
        _add_t_kernel,
        grid=(t // tm,),
        in_specs=[pl.BlockSpec((tm, d), lambda i: (i, 0)),
                  pl.BlockSpec((d, tm), lambda i: (0, i))],
        out_specs=pl.BlockSpec((tm, d), lambda i: (i, 0)),
        out_shape=jax.ShapeDtypeStruct((t, d), F32),
        compiler_params=_params(("parallel",)),
        name="add_transposed",
    )(x, pt)


def _tiles(t, d, in_w):
    def pick(n, cap):
        b = cap
        while n % b:
            b //= 2
        return b
    return dict(
        norm_tm=pick(t, 512),
        mm_tm=pick(t, 1024),
        mm_tn=pick(d // 4, 1024),
        conv_tm=pick(t, 512),
        peer_tm=pick(t, 512),
        score_tm=pick(t, 1024),
        peer_eb=1024,
        cast_tr=256,
    )


def _layer(x2, seq, norm1_g, w_in, q_norm_g, k_norm_g, sink_logits, conv_w, w_o_attn,
           w_o_conv, w_out, norm2_g, w_q_peer, sub_keys, w_down, w_up):
    t, d = x2.shape
    in_w = w_in.shape[1]
    q_w = w_o_attn.shape[0]
    kv_w = q_w // GQA_GROUP
    conv_width = w_o_conv.shape[0]
    off_h = q_w + 2 * kv_w
    off_ga = off_h + 3 * conv_width
    off_gc = off_ga + d
    tl = _tiles(t, d, in_w)

    xn = _rmsnorm(x2, norm1_g, tm=tl["norm_tm"])
    proj = _matmul(xn, w_in.astype(BF16), tm=tl["mm_tm"], tn=tl["mm_tn"],
                   out_dtype=BF16, name="in_proj")
    attn = _attention(proj, sink_logits.astype(F32), q_norm_g, k_norm_g,
                      seq=seq, q_w=q_w, kv_w=kv_w)
    conv = _short_conv(proj, conv_w, seq=seq, off_h=off_h, conv_width=conv_width,
                       tm=tl["conv_tm"])
    merged = _merge(attn, conv, w_o_attn.astype(BF16), w_o_conv.astype(BF16), proj,
                    off_ga=off_ga, off_gc=off_gc, tm=tl["mm_tm"], tn=tl["mm_tn"])
    x1 = _matmul_residual(x2, merged, w_out.astype(BF16), tm=tl["mm_tm"], tn=tl["mm_tn"])

    xn2_t = _rmsnorm(x1, norm2_g, tm=tl["norm_tm"], transpose=True)
    wq_t = _cast_transpose(w_q_peer, tr=tl["cast_tr"])
    rank2, e2, cnt, c = _peer_score(xn2_t, wq_t, sub_keys.astype(BF16),
                                    tm=tl["score_tm"])
    w_up_t = _cast_transpose(w_up, tr=tl["cast_tr"])
    peer_t = _peer_mix(xn2_t, w_down.astype(BF16), w_up_t, rank2, e2, cnt, c,
                       tm=tl["peer_tm"], eb=tl["peer_eb"])
    return _add_transposed(x1, peer_t, tm=tl["norm_tm"])


def kernel(x, norm1_g, w_in, q_norm_g, k_norm_g, sink_logits, conv_w, w_o_attn, w_o_conv,
           w_out, norm2_g, w_q_peer, sub_keys, w_down, w_up):
    b, s, d = x.shape
    x2 = x.reshape(b * s, d)
    for i in range(norm1_g.shape[0]):
        x2 = _layer(x2, s, norm1_g[i], w_in[i], q_norm_g[i], k_norm_g[i], sink_logits[i],
                    conv_w[i], w_o_attn[i], w_o_conv[i], w_out[i], norm2_g[i],
                    w_q_peer[i], sub_keys[i], w_down[i], w_up[i])
    return x2.reshape(b, s, d)
```

```python
import functools

import jax
import jax.numpy as jnp
from jax import lax
from jax.experimental import pallas as pl
from jax.experimental.pallas import tpu as pltpu

F32 = jnp.float32
BF16 = jnp.bfloat16

HEAD_DIM = 128
GQA_GROUP = 4
WINDOW = 128
BLOCK = 128
CONV_K = 3
PEER_HEADS = 8
N_KEYS = 128
PEER_KEY_DIM = 128
PEER_TOPK = 16
EPS = 1e-6
SUBLANES = 8
LANES = 128
BF16_COLS = 2 * LANES
MIX_ACC_VREGS = 32
VMEM_LIMIT = 56 * 1024 * 1024
PEER_MIX_VMEM_LIMIT = 60 * 1024 * 1024


def _params(sem, vmem_limit=VMEM_LIMIT):
    return pltpu.CompilerParams(dimension_semantics=sem, vmem_limit_bytes=vmem_limit)


def _rmsnorm_kernel(x_ref, g_ref, o_ref, *, transpose):
    x = x_ref[...]
    ms = jnp.mean(x * x, axis=-1, keepdims=True)
    y = x * lax.rsqrt(ms + EPS) * g_ref[...]
    if transpose:
        y = y.T
    o_ref[...] = y.astype(o_ref.dtype)


def _rmsnorm(x, g, *, tm, transpose=False):
    t, d = x.shape
    if transpose:
        out_shape = jax.ShapeDtypeStruct((d, t), BF16)
        out_spec = pl.BlockSpec((d, tm), lambda i: (0, i))
    else:
        out_shape = jax.ShapeDtypeStruct((t, d), BF16)
        out_spec = pl.BlockSpec((tm, d), lambda i: (i, 0))
    return pl.pallas_call(
        functools.partial(_rmsnorm_kernel, transpose=transpose),
        grid=(t // tm,),
        in_specs=[pl.BlockSpec((tm, d), lambda i: (i, 0)),
                  pl.BlockSpec((1, d), lambda i: (0, 0))],
        out_specs=out_spec,
        out_shape=out_shape,
        compiler_params=_params(("parallel",)),
        name="rmsnorm_t" if transpose else "rmsnorm",
    )(x, g.reshape(1, d))


def _cast_t_kernel(x_ref, o_ref):
    o_ref[...] = x_ref[...].T.astype(o_ref.dtype)


def _cast_transpose(w, *, tr):
    r, c = w.shape
    return pl.pallas_call(
        _cast_t_kernel,
        grid=(r // tr,),
        in_specs=[pl.BlockSpec((tr, c), lambda i: (i, 0))],
        out_specs=pl.BlockSpec((c, tr), lambda i: (0, i)),
        out_shape=jax.ShapeDtypeStruct((c, r), BF16),
        compiler_params=_params(("parallel",)),
        name="cast_transpose",
    )(w)


def _mm_kernel(a_ref, b_ref, o_ref):
    o_ref[...] = jnp.dot(a_ref[...], b_ref[...],
                         preferred_element_type=F32).astype(o_ref.dtype)


def _matmul(a, b, *, tm, tn, out_dtype, name):
    m, k = a.shape
    n = b.shape[1]
    return pl.pallas_call(
        _mm_kernel,
        grid=(m // tm, n // tn),
        in_specs=[pl.BlockSpec((tm, k), lambda i, j: (i, 0)),
                  pl.BlockSpec((k, tn), lambda i, j: (0, j))],
        out_specs=pl.BlockSpec((tm, tn), lambda i, j: (i, j)),
        out_shape=jax.ShapeDtypeStruct((m, n), out_dtype),
        compiler_params=_params(("parallel", "arbitrary")),
        name=name,
    )(a, b)


def _mm_res_kernel(r_ref, a_ref, b_ref, o_ref):
    o_ref[...] = r_ref[...] + jnp.dot(a_ref[...], b_ref[...], preferred_element_type=F32)


def _matmul_residual(r, a, b, *, tm, tn):
    m, k = a.shape
    n = b.shape[1]
    return pl.pallas_call(
        _mm_res_kernel,
        grid=(m // tm, n // tn),
        in_specs=[pl.BlockSpec((tm, tn), lambda i, j: (i, j)),
                  pl.BlockSpec((tm, k), lambda i, j: (i, 0)),
                  pl.BlockSpec((k, tn), lambda i, j: (0, j))],
        out_specs=pl.BlockSpec((tm, tn), lambda i, j: (i, j)),
        out_shape=jax.ShapeDtypeStruct((m, n), F32),
        compiler_params=_params(("parallel", "arbitrary")),
        name="out_proj_residual",
    )(r, a, b)


def _merge_kernel(attn_ref, conv_ref, woa_ref, woc_ref, ga_ref, gc_ref, o_ref):
    a = jnp.dot(attn_ref[...], woa_ref[...], preferred_element_type=F32)
    c = jnp.dot(conv_ref[...], woc_ref[...], preferred_element_type=F32)
    ga = jax.nn.sigmoid(ga_ref[...].astype(F32))
    gc = jax.nn.sigmoid(gc_ref[...].astype(F32))
    o_ref[...] = (ga * a + gc * c).astype(o_ref.dtype)


def _merge(attn, conv, woa, woc, proj, *, off_ga, off_gc, tm, tn):
    t, qw = attn.shape
    cw = conv.shape[1]
    d = woa.shape[1]
    ga_blk, gc_blk = off_ga // tn, off_gc // tn
    return pl.pallas_call(
        _merge_kernel,
        grid=(t // tm, d // tn),
        in_specs=[pl.BlockSpec((tm, qw), lambda i, j: (i, 0)),
                  pl.BlockSpec((tm, cw), lambda i, j: (i, 0)),
                  pl.BlockSpec((qw, tn), lambda i, j: (0, j)),
                  pl.BlockSpec((cw, tn), lambda i, j: (0, j)),
                  pl.BlockSpec((tm, tn), lambda i, j: (i, ga_blk + j)),
                  pl.BlockSpec((tm, tn), lambda i, j: (i, gc_blk + j))],
        out_specs=pl.BlockSpec((tm, tn), lambda i, j: (i, j)),
        out_shape=jax.ShapeDtypeStruct((t, d), BF16),
        compiler_params=_params(("parallel", "arbitrary")),
        name="merge_proj",
    )(attn, conv, woa, woc, proj, proj)


def _head_rmsnorm(x, g):
    ms = jnp.mean(x * x, axis=-1, keepdims=True)
    return x * lax.rsqrt(ms + EPS) * g


def _attn_kernel(sink_ref, q_ref, kp_ref, kc_ref, kn_ref, vp_ref, vc_ref, vn_ref,
                 qg_ref, kg_ref, o_ref, *, nb, n_kv, n_heads):
    n = pl.program_id(0) % nb
    has_prev = n > 0
    has_next = n < nb - 1
    qi = lax.broadcasted_iota(jnp.int32, (BLOCK, 3 * BLOCK), 0)
    kj = lax.broadcasted_iota(jnp.int32, (BLOCK, 3 * BLOCK), 1)
    dist = jnp.abs(kj - BLOCK - qi)
    valid = ((dist <= WINDOW)
             & ((kj >= BLOCK) | has_prev)
             & ((kj < 2 * BLOCK) | has_next))
    distf = dist.astype(F32)
    qg = qg_ref[...]
    kg = kg_ref[...]
    scale = HEAD_DIM ** -0.5
    heads = range(n_heads)
    cols = [slice(h * HEAD_DIM, (h + 1) * HEAD_DIM) for h in heads]
    kbs = [_head_rmsnorm(jnp.concatenate([kp_ref[:, cols[kv]], kc_ref[:, cols[kv]],
                                          kn_ref[:, cols[kv]]], axis=0).astype(F32),
                         kg).astype(BF16) for kv in range(n_kv)]
    vs = [jnp.concatenate([vp_ref[:, cols[kv]], vc_ref[:, cols[kv]], vn_ref[:, cols[kv]]],
                          axis=0) for kv in range(n_kv)]
    qbs = [_head_rmsnorm(q_ref[:, cols[h]].astype(F32), qg).astype(BF16) for h in heads]
    ss = [lax.dot_general(qbs[h], kbs[h // GQA_GROUP], (((1,), (1,)), ((), ())),
                          preferred_element_type=F32) for h in heads]
    ss = [jnp.where(valid, ss[h] * scale - 2.0 ** (-8.0 * (h + 1) / n_heads) * distf, -jnp.inf)
          for h in heads]
    sinks = [sink_ref[h] for h in heads]
    ms = [jnp.maximum(jnp.max(ss[h], axis=-1, keepdims=True), sinks[h]) for h in heads]
    ps = [jnp.exp(ss[h] - ms[h]) for h in heads]
    invs = [1.0 / (jnp.sum(ps[h], axis=-1, keepdims=True) + jnp.exp(sinks[h] - ms[h]))
            for h in heads]
    for h in heads:
        probs = (ps[h] * invs[h]).astype(BF16)
        o = jnp.dot(probs, vs[h // GQA_GROUP], preferred_element_type=F32)
        o_ref[:, cols[h]] = o.astype(o_ref.dtype)


def _attention(proj, sink, qg, kg, *, seq, q_w, kv_w):
    t = proj.shape[0]
    nb = seq // BLOCK
    n_heads = q_w // HEAD_DIM
    n_kv = kv_w // HEAD_DIM
    k_blk = q_w // kv_w
    v_blk = k_blk + 1

    def prev_row(r):
        return jnp.where(r % nb > 0, r - 1, r)

    def next_row(r):
        return jnp.where(r % nb < nb - 1, r + 1, r)

    kv_specs = [pl.BlockSpec((BLOCK, kv_w), lambda r, c=c, f=f: (f(r), c))
                for c in (k_blk, v_blk) for f in (prev_row, lambda r: r, next_row)]
    return pl.pallas_call(
        functools.partial(_attn_kernel, nb=nb, n_kv=n_kv, n_heads=n_heads),
        grid=(t // BLOCK,),
        in_specs=[pl.BlockSpec(memory_space=pltpu.SMEM),
                  pl.BlockSpec((BLOCK, q_w), lambda r: (r, 0))]
                 + kv_specs
                 + [pl.BlockSpec((1, HEAD_DIM), lambda r: (0, 0)),
                    pl.BlockSpec((1, HEAD_DIM), lambda r: (0, 0))],
        out_specs=pl.BlockSpec((BLOCK, q_w), lambda r: (r, 0)),
        out_shape=jax.ShapeDtypeStruct((t, q_w), BF16),
        compiler_params=_params(("parallel",)),
        name="banded_attention",
    )(sink, proj, proj, proj, proj, proj, proj, proj,
      qg.reshape(1, HEAD_DIM), kg.reshape(1, HEAD_DIM))


def _conv_kernel(h_ref, b_ref, c_ref, hp_ref, cp_ref, hn_ref, cn_ref, w_ref, o_ref,
                 *, tiles_per_seq):
    i = pl.program_id(0)
    tm = h_ref.shape[0]
    u = c_ref[...].astype(F32) * h_ref[...].astype(F32)
    first = (i % tiles_per_seq) == 0
    last = (i % tiles_per_seq) == tiles_per_seq - 1
    u_prev = (cp_ref[SUBLANES - 1:SUBLANES, :].astype(F32)
              * hp_ref[SUBLANES - 1:SUBLANES, :].astype(F32))
    u_next = cn_ref[0:1, :].astype(F32) * hn_ref[0:1, :].astype(F32)
    u_prev = jnp.where(first, 0.0, u_prev)
    u_next = jnp.where(last, 0.0, u_next)
    row = lax.broadcasted_iota(jnp.int32, u.shape, 0)
    u_m1 = jnp.where(row == 0, u_prev, pltpu.roll(u, 1, axis=0))
    u_p1 = jnp.where(row == tm - 1, u_next, pltpu.roll(u, tm - 1, axis=0))
    w = w_ref[...]
    y = w[0:1, :] * u_m1 + w[1:2, :] * u + w[2:3, :] * u_p1
    o_ref[...] = (b_ref[...].astype(F32) * y).astype(o_ref.dtype)


def _short_conv(proj, conv_w, *, seq, off_h, conv_width, tm):
    t = proj.shape[0]
    cw = conv_width // 2
    nc = conv_width // cw
    hb, bb, cb = off_h // cw, (off_h + conv_width) // cw, (off_h + 2 * conv_width) // cw
    rows8 = tm // SUBLANES
    last8 = t // SUBLANES - 1

    def prev8(i):
        return jnp.maximum(i * rows8 - 1, 0)

    def next8(i):
        return jnp.minimum((i + 1) * rows8, last8)

    w_pad = jnp.zeros((SUBLANES, conv_width), F32).at[:CONV_K].set(conv_w)
    return pl.pallas_call(
        functools.partial(_conv_kernel, tiles_per_seq=seq // tm),
        grid=(t // tm, nc),
        in_specs=[pl.BlockSpec((tm, cw), lambda i, j: (i, hb + j)),
                  pl.BlockSpec((tm, cw), lambda i, j: (i, bb + j)),
                  pl.BlockSpec((tm, cw), lambda i, j: (i, cb + j)),
                  pl.BlockSpec((SUBLANES, cw), lambda i, j: (prev8(i), hb + j)),
                  pl.BlockSpec((SUBLANES, cw), lambda i, j: (prev8(i), cb + j)),
                  pl.BlockSpec((SUBLANES, cw), lambda i, j: (next8(i), hb + j)),
                  pl.BlockSpec((SUBLANES, cw), lambda i, j: (next8(i), cb + j)),
                  pl.BlockSpec((SUBLANES, cw), lambda i, j: (0, j))],
        out_specs=pl.BlockSpec((tm, cw), lambda i, j: (i, j)),
        out_shape=jax.ShapeDtypeStruct((t, conv_width), BF16),
        compiler_params=_params(("parallel", "arbitrary")),
        name="short_conv",
    )(proj, proj, proj, proj, proj, proj, proj, w_pad)


def _extract_top(work, count, *, break_ties):
    n = work.shape[0]
    iota = lax.broadcasted_iota(jnp.int32, work.shape, 0).astype(F32)
    rank = jnp.full(work.shape, float(count), F32)
    vals = []
    for r in range(count):
        m = jnp.max(work, axis=0, keepdims=True)
        vals.append(m)
        hit = work == m
        if break_ties:
            first = jnp.min(jnp.where(hit, iota, float(n)), axis=0, keepdims=True)
            hit = iota == first
        work = jnp.where(hit, -jnp.inf, work)
        rank = jnp.where(hit, float(r), rank)
    return vals, rank


def _rank_and_gate(s1, s2, *, break_ties):
    tm = s1.shape[1]
    v12, rank12 = _extract_top(jnp.concatenate([s1, s2], axis=1), PEER_TOPK,
                               break_ties=break_ties)
    v1, v2 = [v[:, :tm] for v in v12], [v[:, tm:] for v in v12]
    rank1, rank2 = rank12[:, :tm], rank12[:, tm:]

    v2_all = jnp.concatenate(v2, axis=0)
    v2_low = v2_all[:SUBLANES]
    row = lax.broadcasted_iota(jnp.int32, (SUBLANES, tm), 0)
    pieces = [v1[0] + v2_all]
    for a in range(1, SUBLANES):
        nb = PEER_TOPK // (a + 1)
        piece = v1[a] + v2_low
        pieces.append(piece if nb >= SUBLANES else jnp.where(row < nb, piece, -jnp.inf))
    pieces.append(jnp.concatenate(v1[SUBLANES:], axis=0) + v2[0])
    cand = jnp.concatenate(pieces, axis=0)
    top, cand_rank = _extract_top(cand, PEER_TOPK, break_ties=break_ties)
    chosen = jnp.where(cand_rank < PEER_TOPK, 1.0, 0.0)

    cnt_a = [jnp.sum(chosen[:PEER_TOPK], axis=0, keepdims=True)]
    for a in range(1, SUBLANES):
        lo = PEER_TOPK + (a - 1) * SUBLANES
        cnt_a.append(jnp.sum(chosen[lo:lo + SUBLANES], axis=0, keepdims=True))
    lo = PEER_TOPK + (SUBLANES - 1) * SUBLANES
    cnt_a += [chosen[lo + k:lo + k + 1] for k in range(PEER_TOPK - SUBLANES)]
    cnt = jnp.zeros_like(s1)
    for a in range(PEER_TOPK):
        cnt = jnp.where(rank1 == float(a), cnt_a[a], cnt)

    m1, m2 = v1[0], v2[0]
    denom = jnp.exp(top[0] - (m1 + m2))
    for r in range(1, PEER_TOPK):
        denom = denom + jnp.exp(top[r] - (m1 + m2))
    e2 = jnp.exp(s2 - m2)
    c = jnp.exp(s1 - m1) / denom

    n_ranked = jnp.sum(jnp.where(rank12 < PEER_TOPK, 1.0, 0.0), axis=0, keepdims=True)
    n_chosen = sum(cnt_a[1:], cnt_a[0])
    most = jnp.maximum(jnp.max(n_ranked), jnp.max(n_chosen))
    return rank2, e2, cnt, c, most


def _peer_score_kernel(xt_ref, wq_ref, keys_ref, r2_ref, e2_ref, cnt_ref, c_ref):
    qt = jnp.dot(wq_ref[...], xt_ref[...], preferred_element_type=F32)
    q1 = qt[:PEER_KEY_DIM].astype(BF16)
    q2 = qt[PEER_KEY_DIM:].astype(BF16)
    s1 = jnp.dot(keys_ref[0, 0], q1, preferred_element_type=F32)
    s2 = jnp.dot(keys_ref[0, 1], q2, preferred_element_type=F32)
    tm = s1.shape[1]

    def write(rank2, e2, cnt, c):
        r2_ref[...] = rank2.astype(r2_ref.dtype)
        e2_ref[...] = e2.astype(e2_ref.dtype)
        cnt_ref[:, 0] = cnt.reshape(N_KEYS // SUBLANES, SUBLANES, tm)
        c_ref[:, 0] = c.reshape(N_KEYS // SUBLANES, SUBLANES, tm)

    *result, most = _rank_and_gate(s1, s2, break_ties=False)
    write(*result)

    @pl.when(most > PEER_TOPK)
    def _():
        write(*_rank_and_gate(s1, s2, break_ties=True)[:4])


def _peer_score(xt, wq_t, keys, *, tm):
    d, t = xt.shape
    qd = 2 * PEER_KEY_DIM
    groups = N_KEYS // SUBLANES
    dense = jax.ShapeDtypeStruct((PEER_HEADS * N_KEYS, t), BF16)
    grouped = jax.ShapeDtypeStruct((groups, PEER_HEADS, SUBLANES, t), F32)
    dense_spec = pl.BlockSpec((N_KEYS, tm), lambda i, h: (h, i))
    grouped_spec = pl.BlockSpec((groups, 1, SUBLANES, tm), lambda i, h: (0, h, 0, i))
    return pl.pallas_call(
        _peer_score_kernel,
        grid=(t // tm, PEER_HEADS),
        in_specs=[pl.BlockSpec((d, tm), lambda i, h: (0, i)),
                  pl.BlockSpec((qd, d), lambda i, h: (h, 0)),
                  pl.BlockSpec((1, 2, N_KEYS, PEER_KEY_DIM), lambda i, h: (h, 0, 0, 0))],
        out_specs=[dense_spec, dense_spec, grouped_spec, grouped_spec],
        out_shape=[dense, dense, grouped, grouped],
        compiler_params=_params(("parallel", "arbitrary")),
        name="peer_score",
    )(xt, wq_t, keys)


def _gelu(x):
    return 0.5 * x * (1.0 + lax.erf(x * (2.0 ** -0.5)))


def _build_gate_weights(r2_ref, e2_ref, cnt_ref, c_ref, w_ref, rows_ref, sub, n_i):
    tm = w_ref.shape[1]
    groups_per_tile = SUBLANES // n_i
    chunk_rows = SUBLANES * MIX_ACC_VREGS // n_i
    for h in range(PEER_HEADS):
        for ii in range(n_i):
            cnt_row = cnt_ref[0, h, ii:ii + 1, :]
            c_row = c_ref[0, h, ii:ii + 1, :]
            for alt in range(1, groups_per_tile):
                r = alt * n_i + ii
                cnt_row = jnp.where(sub == alt, cnt_ref[0, h, r:r + 1, :], cnt_row)
                c_row = jnp.where(sub == alt, c_ref[0, h, r:r + 1, :], c_row)
            rows_ref[0, h * n_i + ii] = jnp.broadcast_to(cnt_row, (SUBLANES, tm)).astype(BF16)
            rows_ref[1, h * n_i + ii] = jnp.broadcast_to(c_row, (SUBLANES, tm)).astype(BF16)
    for tl in range(tm // BF16_COLS):
        cols = slice(tl * BF16_COLS, (tl + 1) * BF16_COLS)
        for jc in range(N_KEYS // chunk_rows):
            rows = slice(jc * chunk_rows, (jc + 1) * chunk_rows)
            acc = [None] * n_i
            packed = (chunk_rows // SUBLANES, SUBLANES, BF16_COLS)
            for h in range(PEER_HEADS):
                hrows = slice(h * N_KEYS + rows.start, h * N_KEYS + rows.stop)
                r2c = r2_ref[hrows, cols].reshape(packed)
                e2c = e2_ref[hrows, cols].reshape(packed)
                for ii in range(n_i):
                    cnt_b = rows_ref[0, h * n_i + ii, :, cols][None]
                    c_b = rows_ref[1, h * n_i + ii, :, cols][None]
                    term = jnp.where(r2c < cnt_b, e2c * c_b, jnp.zeros((), BF16))
                    acc[ii] = term if acc[ii] is None else acc[ii] + term
            for ii in range(n_i):
                r0 = ii * N_KEYS + jc * chunk_rows
                w_ref[r0:r0 + chunk_rows, cols] = acc[ii].reshape(chunk_rows, BF16_COLS)


def _peer_mix_kernel(xt_ref, wd_ref, wu_ref, r2_ref, e2_ref, cnt_ref, c_ref, o_ref, w_ref,
                     rows_ref, *, n_i):
    e = pl.program_id(1)

    @pl.when(e == 0)
    def _():
        o_ref[...] = jnp.zeros_like(o_ref)

    _build_gate_weights(r2_ref, e2_ref, cnt_ref, c_ref, w_ref, rows_ref,
                        e % (SUBLANES // n_i), n_i)
    a = jnp.dot(wd_ref[...], xt_ref[...], preferred_element_type=F32)
    gated = _gelu(a).astype(BF16) * w_ref[...]
    o_ref[...] += jnp.dot(wu_ref[...], gated, preferred_element_type=F32)


def _peer_mix(xt, w_down, w_up_t, rank2, e2, cnt, c, *, tm, eb):
    d, t = xt.shape
    ne = w_down.shape[0]
    n_i = eb // N_KEYS
    groups_per_tile = SUBLANES // n_i
    once = pl.Buffered(1)
    dense_spec = pl.BlockSpec((PEER_HEADS * N_KEYS, tm), lambda i, e: (0, i),
                              pipeline_mode=once)
    grouped_spec = pl.BlockSpec((1, PEER_HEADS, SUBLANES, tm),
                                lambda i, e: (e // groups_per_tile, 0, 0, i))
    return pl.pallas_call(
        functools.partial(_peer_mix_kernel, n_i=n_i),
        grid=(t // tm, ne // eb),
        in_specs=[pl.BlockSpec((d, tm), lambda i, e: (0, i), pipeline_mode=once),
                  pl.BlockSpec((eb, d), lambda i, e: (e, 0)),
                  pl.BlockSpec((d, eb), lambda i, e: (0, e)),
                  dense_spec, dense_spec, grouped_spec, grouped_spec],
        out_specs=pl.BlockSpec((d, tm), lambda i, e: (0, i), pipeline_mode=once),
        out_shape=jax.ShapeDtypeStruct((d, t), F32),
        scratch_shapes=[pltpu.VMEM((eb, tm), BF16),
                        pltpu.VMEM((2, PEER_HEADS * n_i, SUBLANES, tm), BF16)],
        compiler_params=_params(("parallel", "arbitrary"), PEER_MIX_VMEM_LIMIT),
        name="peer_mix",
    )(xt, w_down, w_up_t, rank2, e2, cnt, c)


def _add_t_kernel(x_ref, pt_ref, o_ref):
    o_ref[...] = x_ref[...] + pt_ref[...].T


def _add_transposed(x, pt, *, tm):
    t, d = x.shape
    return pl.pallas_call(
        _add_t_kernel,
        grid=(t // tm,),
        in_specs=[pl.BlockSpec((tm, d), lambda i: (i, 0)),
                  pl.BlockSpec((d, tm), lambda i: (0, i))],
        out_specs=pl.BlockSpec((tm, d), lambda i: (i, 0)),
        out_shape=jax.ShapeDtypeStruct((t, d), F32),
        compiler_params=_params(("parallel",)),
        name="add_transposed",
    )(x, pt)


def _tiles(t, d, in_w):
    def pick(n, cap):
        b = cap
        while n % b:
            b //= 2
        return b
    return dict(
        norm_tm=pick(t, 512),
        mm_tm=pick(t, 1024),
        mm_tn=pick(d // 4, 1024),
        conv_tm=pick(t, 512),
        peer_tm=pick(t, 1024),
        score_tm=pick(t, 1024),
        peer_eb=512,
        cast_tr=256,
    )


def _layer(x2, seq, norm1_g, w_in, q_norm_g, k_norm_g, sink_logits, conv_w, w_o_attn,
           w_o_conv, w_out, norm2_g, w_q_peer, sub_keys, w_down, w_up):
    t, d = x2.shape
    in_w = w_in.shape[1]
    q_w = w_o_attn.shape[0]
    kv_w = q_w // GQA_GROUP
    conv_width = w_o_conv.shape[0]
    off_h = q_w + 2 * kv_w
    off_ga = off_h + 3 * conv_width
    off_gc = off_ga + d
    tl = _tiles(t, d, in_w)

    xn = _rmsnorm(x2, norm1_g, tm=tl["norm_tm"])
    proj = _matmul(xn, w_in.astype(BF16), tm=tl["mm_tm"], tn=tl["mm_tn"],
                   out_dtype=BF16, name="in_proj")
    attn = _attention(proj, sink_logits.astype(F32), q_norm_g, k_norm_g,
                      seq=seq, q_w=q_w, kv_w=kv_w)
    conv = _short_conv(proj, conv_w, seq=seq, off_h=off_h, conv_width=conv_width,
                       tm=tl["conv_tm"])
    merged = _merge(attn, conv, w_o_attn.astype(BF16), w_o_conv.astype(BF16), proj,
                    off_ga=off_ga, off_gc=off_gc, tm=tl["mm_tm"], tn=tl["mm_tn"])
    x1 = _matmul_residual(x2, merged, w_out.astype(BF16), tm=tl["mm_tm"], tn=tl["mm_tn"])

    xn2_t = _rmsnorm(x1, norm2_g, tm=tl["norm_tm"], transpose=True)
    wq_t = _cast_transpose(w_q_peer, tr=tl["cast_tr"])
    rank2, e2, cnt, c = _peer_score(xn2_t, wq_t, sub_keys.astype(BF16),
                                    tm=tl["score_tm"])
    w_up_t = _cast_transpose(w_up, tr=tl["cast_tr"])
    peer_t = _peer_mix(xn2_t, w_down.astype(BF16), w_up_t, rank2, e2, cnt, c,
                       tm=tl["peer_tm"], eb=tl["peer_eb"])
    return _add_transposed(x1, peer_t, tm=tl["norm_tm"])


def kernel(x, norm1_g, w_in, q_norm_g, k_norm_g, sink_logits, conv_w, w_o_attn, w_o_conv,
           w_out, norm2_g, w_q_peer, sub_keys, w_down, w_up):
    b, s, d = x.shape
    x2 = x.reshape(b * s, d)
    for i in range(norm1_g.shape[0]):
        x2 = _layer(x2, s, norm1_g[i], w_in[i], q_norm_g[i], k_norm_g[i], sink_logits[i],
                    conv_w[i], w_o_attn[i], w_o_conv[i], w_out[i], norm2_g[i],
                    w_q_peer[i], sub_keys[i], w_down[i], w_up[i])
    return x2.reshape(b, s, d)
```

```python
import functools

import jax
import jax.numpy as jnp
from jax import lax
from jax.experimental import pallas as pl
from jax.experimental.pallas import tpu as pltpu

F32 = jnp.float32
BF16 = jnp.bfloat16

HEAD_DIM = 128
GQA_GROUP = 4
WINDOW = 128
BLOCK = 128
CONV_K = 3
PEER_HEADS = 8
N_KEYS = 128
PEER_KEY_DIM = 128
PEER_TOPK = 16
EPS = 1e-6
SUBLANES = 8
LANES = 128
BF16_COLS = 2 * LANES
MIX_ACC_VREGS = 32
SCORE_COLS = 2 * LANES
VMEM_LIMIT = 56 * 1024 * 1024


def _params(sem):
    return pltpu.CompilerParams(dimension_semantics=sem, vmem_limit_bytes=VMEM_LIMIT)


def _rmsnorm_kernel(x_ref, g_ref, o_ref, *, transpose):
    x = x_ref[...]
    ms = jnp.mean(x * x, axis=-1, keepdims=True)
    y = x * lax.rsqrt(ms + EPS) * g_ref[...]
    if transpose:
        y = y.T
    o_ref[...] = y.astype(o_ref.dtype)


def _rmsnorm(x, g, *, tm, transpose=False):
    t, d = x.shape
    if transpose:
        out_shape = jax.ShapeDtypeStruct((d, t), BF16)
        out_spec = pl.BlockSpec((d, tm), lambda i: (0, i))
    else:
        out_shape = jax.ShapeDtypeStruct((t, d), BF16)
        out_spec = pl.BlockSpec((tm, d), lambda i: (i, 0))
    return pl.pallas_call(
        functools.partial(_rmsnorm_kernel, transpose=transpose),
        grid=(t // tm,),
        in_specs=[pl.BlockSpec((tm, d), lambda i: (i, 0)),
                  pl.BlockSpec((1, d), lambda i: (0, 0))],
        out_specs=out_spec,
        out_shape=out_shape,
        compiler_params=_params(("parallel",)),
        name="rmsnorm_t" if transpose else "rmsnorm",
    )(x, g.reshape(1, d))


def _cast_t_kernel(x_ref, o_ref):
    o_ref[...] = x_ref[...].T.astype(o_ref.dtype)


def _cast_transpose(w, *, tr):
    r, c = w.shape
    return pl.pallas_call(
        _cast_t_kernel,
        grid=(r // tr,),
        in_specs=[pl.BlockSpec((tr, c), lambda i: (i, 0))],
        out_specs=pl.BlockSpec((c, tr), lambda i: (0, i)),
        out_shape=jax.ShapeDtypeStruct((c, r), BF16),
        compiler_params=_params(("parallel",)),
        name="cast_transpose",
    )(w)


def _mm_kernel(a_ref, b_ref, o_ref):
    o_ref[...] = jnp.dot(a_ref[...], b_ref[...],
                         preferred_element_type=F32).astype(o_ref.dtype)


def _matmul(a, b, *, tm, tn, out_dtype, name):
    m, k = a.shape
    n = b.shape[1]
    return pl.pallas_call(
        _mm_kernel,
        grid=(m // tm, n // tn),
        in_specs=[pl.BlockSpec((tm, k), lambda i, j: (i, 0)),
                  pl.BlockSpec((k, tn), lambda i, j: (0, j))],
        out_specs=pl.BlockSpec((tm, tn), lambda i, j: (i, j)),
        out_shape=jax.ShapeDtypeStruct((m, n), out_dtype),
        compiler_params=_params(("parallel", "arbitrary")),
        name=name,
    )(a, b)


def _mm_res_kernel(r_ref, a_ref, b_ref, o_ref):
    o_ref[...] = r_ref[...] + jnp.dot(a_ref[...], b_ref[...], preferred_element_type=F32)


def _matmul_residual(r, a, b, *, tm, tn):
    m, k = a.shape
    n = b.shape[1]
    return pl.pallas_call(
        _mm_res_kernel,
        grid=(m // tm, n // tn),
        in_specs=[pl.BlockSpec((tm, tn), lambda i, j: (i, j)),
                  pl.BlockSpec((tm, k), lambda i, j: (i, 0)),
                  pl.BlockSpec((k, tn), lambda i, j: (0, j))],
        out_specs=pl.BlockSpec((tm, tn), lambda i, j: (i, j)),
        out_shape=jax.ShapeDtypeStruct((m, n), F32),
        compiler_params=_params(("parallel", "arbitrary")),
        name="out_proj_residual",
    )(r, a, b)


def _merge_kernel(attn_ref, conv_ref, woa_ref, woc_ref, ga_ref, gc_ref, o_ref):
    a = jnp.dot(attn_ref[...], woa_ref[...], preferred_element_type=F32)
    c = jnp.dot(conv_ref[...], woc_ref[...], preferred_element_type=F32)
    ga = jax.nn.sigmoid(ga_ref[...].astype(F32))
    gc = jax.nn.sigmoid(gc_ref[...].astype(F32))
    o_ref[...] = (ga * a + gc * c).astype(o_ref.dtype)


def _merge(attn, conv, woa, woc, proj, *, off_ga, off_gc, tm, tn):
    t, qw = attn.shape
    cw = conv.shape[1]
    d = woa.shape[1]
    ga_blk, gc_blk = off_ga // tn, off_gc // tn
    return pl.pallas_call(
        _merge_kernel,
        grid=(t // tm, d // tn),
        in_specs=[pl.BlockSpec((tm, qw), lambda i, j: (i, 0)),
                  pl.BlockSpec((tm, cw), lambda i, j: (i, 0)),
                  pl.BlockSpec((qw, tn), lambda i, j: (0, j)),
                  pl.BlockSpec((cw, tn), lambda i, j: (0, j)),
                  pl.BlockSpec((tm, tn), lambda i, j: (i, ga_blk + j)),
                  pl.BlockSpec((tm, tn), lambda i, j: (i, gc_blk + j))],
        out_specs=pl.BlockSpec((tm, tn), lambda i, j: (i, j)),
        out_shape=jax.ShapeDtypeStruct((t, d), BF16),
        compiler_params=_params(("parallel", "arbitrary")),
        name="merge_proj",
    )(attn, conv, woa, woc, proj, proj)


def _head_rmsnorm(x, g):
    ms = jnp.mean(x * x, axis=-1, keepdims=True)
    return x * lax.rsqrt(ms + EPS) * g


def _attn_kernel(sink_ref, q_ref, kp_ref, kc_ref, kn_ref, vp_ref, vc_ref, vn_ref,
                 qg_ref, kg_ref, o_ref, *, nb, n_kv, n_heads):
    n = pl.program_id(0) % nb
    has_prev = n > 0
    has_next = n < nb - 1
    qi = lax.broadcasted_iota(jnp.int32, (BLOCK, 3 * BLOCK), 0)
    kj = lax.broadcasted_iota(jnp.int32, (BLOCK, 3 * BLOCK), 1)
    dist = jnp.abs(kj - BLOCK - qi)
    valid = ((dist <= WINDOW)
             & ((kj >= BLOCK) | has_prev)
             & ((kj < 2 * BLOCK) | has_next))
    distf = dist.astype(F32)
    qg = qg_ref[...]
    kg = kg_ref[...]
    scale = HEAD_DIM ** -0.5
    heads = range(n_heads)
    cols = [slice(h * HEAD_DIM, (h + 1) * HEAD_DIM) for h in heads]
    kbs = [_head_rmsnorm(jnp.concatenate([kp_ref[:, cols[kv]], kc_ref[:, cols[kv]],
                                          kn_ref[:, cols[kv]]], axis=0).astype(F32),
                         kg).astype(BF16) for kv in range(n_kv)]
    vs = [jnp.concatenate([vp_ref[:, cols[kv]], vc_ref[:, cols[kv]], vn_ref[:, cols[kv]]],
                          axis=0) for kv in range(n_kv)]
    qbs = [_head_rmsnorm(q_ref[:, cols[h]].astype(F32), qg).astype(BF16) for h in heads]
    ss = [lax.dot_general(qbs[h], kbs[h // GQA_GROUP], (((1,), (1,)), ((), ())),
                          preferred_element_type=F32) for h in heads]
    ss = [jnp.where(valid, ss[h] * scale - 2.0 ** (-8.0 * (h + 1) / n_heads) * distf, -jnp.inf)
          for h in heads]
    sinks = [sink_ref[h] for h in heads]
    ms = [jnp.maximum(jnp.max(ss[h], axis=-1, keepdims=True), sinks[h]) for h in heads]
    ps = [jnp.exp(ss[h] - ms[h]) for h in heads]
    invs = [1.0 / (jnp.sum(ps[h], axis=-1, keepdims=True) + jnp.exp(sinks[h] - ms[h]))
            for h in heads]
    for h in heads:
        probs = (ps[h] * invs[h]).astype(BF16)
        o = jnp.dot(probs, vs[h // GQA_GROUP], preferred_element_type=F32)
        o_ref[:, cols[h]] = o.astype(o_ref.dtype)


def _attention(proj, sink, qg, kg, *, seq, q_w, kv_w):
    t = proj.shape[0]
    nb = seq // BLOCK
    n_heads = q_w // HEAD_DIM
    n_kv = kv_w // HEAD_DIM
    k_blk = q_w // kv_w
    v_blk = k_blk + 1

    def prev_row(r):
        return jnp.where(r % nb > 0, r - 1, r)

    def next_row(r):
        return jnp.where(r % nb < nb - 1, r + 1, r)

    kv_specs = [pl.BlockSpec((BLOCK, kv_w), lambda r, c=c, f=f: (f(r), c))
                for c in (k_blk, v_blk) for f in (prev_row, lambda r: r, next_row)]
    return pl.pallas_call(
        functools.partial(_attn_kernel, nb=nb, n_kv=n_kv, n_heads=n_heads),
        grid=(t // BLOCK,),
        in_specs=[pl.BlockSpec(memory_space=pltpu.SMEM),
                  pl.BlockSpec((BLOCK, q_w), lambda r: (r, 0))]
                 + kv_specs
                 + [pl.BlockSpec((1, HEAD_DIM), lambda r: (0, 0)),
                    pl.BlockSpec((1, HEAD_DIM), lambda r: (0, 0))],
        out_specs=pl.BlockSpec((BLOCK, q_w), lambda r: (r, 0)),
        out_shape=jax.ShapeDtypeStruct((t, q_w), BF16),
        compiler_params=_params(("parallel",)),
        name="banded_attention",
    )(sink, proj, proj, proj, proj, proj, proj, proj,
      qg.reshape(1, HEAD_DIM), kg.reshape(1, HEAD_DIM))


def _conv_kernel(h_ref, b_ref, c_ref, hp_ref, cp_ref, hn_ref, cn_ref, w_ref, o_ref,
                 *, tiles_per_seq):
    i = pl.program_id(0)
    tm = h_ref.shape[0]
    u = c_ref[...].astype(F32) * h_ref[...].astype(F32)
    first = (i % tiles_per_seq) == 0
    last = (i % tiles_per_seq) == tiles_per_seq - 1
    u_prev = (cp_ref[SUBLANES - 1:SUBLANES, :].astype(F32)
              * hp_ref[SUBLANES - 1:SUBLANES, :].astype(F32))
    u_next = cn_ref[0:1, :].astype(F32) * hn_ref[0:1, :].astype(F32)
    u_prev = jnp.where(first, 0.0, u_prev)
    u_next = jnp.where(last, 0.0, u_next)
    row = lax.broadcasted_iota(jnp.int32, u.shape, 0)
    u_m1 = jnp.where(row == 0, u_prev, pltpu.roll(u, 1, axis=0))
    u_p1 = jnp.where(row == tm - 1, u_next, pltpu.roll(u, tm - 1, axis=0))
    w = w_ref[...]
    y = w[0:1, :] * u_m1 + w[1:2, :] * u + w[2:3, :] * u_p1
    o_ref[...] = (b_ref[...].astype(F32) * y).astype(o_ref.dtype)


def _short_conv(proj, conv_w, *, seq, off_h, conv_width, tm):
    t = proj.shape[0]
    cw = conv_width // 2
    nc = conv_width // cw
    hb, bb, cb = off_h // cw, (off_h + conv_width) // cw, (off_h + 2 * conv_width) // cw
    rows8 = tm // SUBLANES
    last8 = t // SUBLANES - 1

    def prev8(i):
        return jnp.maximum(i * rows8 - 1, 0)

    def next8(i):
        return jnp.minimum((i + 1) * rows8, last8)

    w_pad = jnp.zeros((SUBLANES, conv_width), F32).at[:CONV_K].set(conv_w)
    return pl.pallas_call(
        functools.partial(_conv_kernel, tiles_per_seq=seq // tm),
        grid=(t // tm, nc),
        in_specs=[pl.BlockSpec((tm, cw), lambda i, j: (i, hb + j)),
                  pl.BlockSpec((tm, cw), lambda i, j: (i, bb + j)),
                  pl.BlockSpec((tm, cw), lambda i, j: (i, cb + j)),
                  pl.BlockSpec((SUBLANES, cw), lambda i, j: (prev8(i), hb + j)),
                  pl.BlockSpec((SUBLANES, cw), lambda i, j: (prev8(i), cb + j)),
                  pl.BlockSpec((SUBLANES, cw), lambda i, j: (next8(i), hb + j)),
                  pl.BlockSpec((SUBLANES, cw), lambda i, j: (next8(i), cb + j)),
                  pl.BlockSpec((SUBLANES, cw), lambda i, j: (0, j))],
        out_specs=pl.BlockSpec((tm, cw), lambda i, j: (i, j)),
        out_shape=jax.ShapeDtypeStruct((t, conv_width), BF16),
        compiler_params=_params(("parallel", "arbitrary")),
        name="short_conv",
    )(proj, proj, proj, proj, proj, proj, proj, w_pad)


def _extract_top(work, count, *, break_ties):
    n = work.shape[0]
    iota = lax.broadcasted_iota(jnp.int32, work.shape, 0).astype(F32)
    rank = jnp.full(work.shape, float(count), F32)
    vals = []
    for r in range(count):
        m = jnp.max(work, axis=0, keepdims=True)
        vals.append(m)
        hit = work == m
        if break_ties:
            first = jnp.min(jnp.where(hit, iota, float(n)), axis=0, keepdims=True)
            hit = iota == first
        work = jnp.where(hit, -jnp.inf, work)
        rank = jnp.where(hit, float(r), rank)
    return vals, rank


def _rank_and_gate(s1, s2, *, break_ties, side_by_side=True):
    tm = s1.shape[1]
    if side_by_side:
        v12, rank12 = _extract_top(jnp.concatenate([s1, s2], axis=1), PEER_TOPK,
                                   break_ties=break_ties)
        v1, v2 = [v[:, :tm] for v in v12], [v[:, tm:] for v in v12]
        rank1, rank2 = rank12[:, :tm], rank12[:, tm:]
    else:
        v1, rank1 = _extract_top(s1, PEER_TOPK, break_ties=break_ties)
        v2, rank2 = _extract_top(s2, PEER_TOPK, break_ties=break_ties)
        rank12 = jnp.concatenate([rank1, rank2], axis=1)

    v2_all = jnp.concatenate(v2, axis=0)
    v2_low = v2_all[:SUBLANES]
    row = lax.broadcasted_iota(jnp.int32, (SUBLANES, tm), 0)
    pieces = [v1[0] + v2_all]
    for a in range(1, SUBLANES):
        nb = PEER_TOPK // (a + 1)
        piece = v1[a] + v2_low
        pieces.append(piece if nb >= SUBLANES else jnp.where(row < nb, piece, -jnp.inf))
    pieces.append(jnp.concatenate(v1[SUBLANES:], axis=0) + v2[0])
    cand = jnp.concatenate(pieces, axis=0)
    top, cand_rank = _extract_top(cand, PEER_TOPK, break_ties=break_ties)
    chosen = jnp.where(cand_rank < PEER_TOPK, 1.0, 0.0)

    cnt_a = [jnp.sum(chosen[:PEER_TOPK], axis=0, keepdims=True)]
    for a in range(1, SUBLANES):
        lo = PEER_TOPK + (a - 1) * SUBLANES
        cnt_a.append(jnp.sum(chosen[lo:lo + SUBLANES], axis=0, keepdims=True))
    lo = PEER_TOPK + (SUBLANES - 1) * SUBLANES
    cnt_a += [chosen[lo + k:lo + k + 1] for k in range(PEER_TOPK - SUBLANES)]
    cnt = jnp.zeros_like(s1)
    for a in range(PEER_TOPK):
        cnt = jnp.where(rank1 == float(a), cnt_a[a], cnt)

    m1, m2 = v1[0], v2[0]
    denom = jnp.exp(top[0] - (m1 + m2))
    for r in range(1, PEER_TOPK):
        denom = denom + jnp.exp(top[r] - (m1 + m2))
    e2 = jnp.exp(s2 - m2)
    c = jnp.exp(s1 - m1) / denom

    n_ranked = jnp.sum(jnp.where(rank12 < PEER_TOPK, 1.0, 0.0), axis=0, keepdims=True)
    n_chosen = sum(cnt_a[1:], cnt_a[0])
    most = jnp.maximum(jnp.max(n_ranked), jnp.max(n_chosen))
    return rank2, e2, cnt, c, most


def _peer_score_kernel(xt_ref, wq_ref, keys_ref, r2_ref, e2_ref, cnt_ref, c_ref):
    qt = jnp.dot(wq_ref[...], xt_ref[...], preferred_element_type=F32)
    q1 = qt[:PEER_KEY_DIM].astype(BF16)
    q2 = qt[PEER_KEY_DIM:].astype(BF16)
    s1 = jnp.dot(keys_ref[0, 0], q1, preferred_element_type=F32)
    s2 = jnp.dot(keys_ref[0, 1], q2, preferred_element_type=F32)
    tm = s1.shape[1]

    def write(cs, rank2, e2, cnt, c):
        w = cs.stop - cs.start
        r2_ref[:, cs] = rank2.astype(r2_ref.dtype)
        e2_ref[:, cs] = e2.astype(e2_ref.dtype)
        cnt_ref[:, 0, :, cs] = cnt.reshape(N_KEYS // SUBLANES, SUBLANES, w)
        c_ref[:, 0, :, cs] = c.reshape(N_KEYS // SUBLANES, SUBLANES, w)

    most = None
    for p in range(tm // SCORE_COLS):
        cs = slice(p * SCORE_COLS, (p + 1) * SCORE_COLS)
        *result, m = _rank_and_gate(s1[:, cs], s2[:, cs], break_ties=False, side_by_side=False)
        write(cs, *result)
        most = m if most is None else jnp.maximum(most, m)

    @pl.when(most > PEER_TOPK)
    def _():
        write(slice(0, tm), *_rank_and_gate(s1, s2, break_ties=True)[:4])


def _peer_score(xt, wq_t, keys, *, tm):
    d, t = xt.shape
    qd = 2 * PEER_KEY_DIM
    groups = N_KEYS // SUBLANES
    dense = jax.ShapeDtypeStruct((PEER_HEADS * N_KEYS, t), BF16)
    grouped = jax.ShapeDtypeStruct((groups, PEER_HEADS, SUBLANES, t), F32)
    dense_spec = pl.BlockSpec((N_KEYS, tm), lambda i, h: (h, i))
    grouped_spec = pl.BlockSpec((groups, 1, SUBLANES, tm), lambda i, h: (0, h, 0, i))
    return pl.pallas_call(
        _peer_score_kernel,
        grid=(t // tm, PEER_HEADS),
        in_specs=[pl.BlockSpec((d, tm), lambda i, h: (0, i)),
                  pl.BlockSpec((qd, d), lambda i, h: (h, 0)),
                  pl.BlockSpec((1, 2, N_KEYS, PEER_KEY_DIM), lambda i, h: (h, 0, 0, 0))],
        out_specs=[dense_spec, dense_spec, grouped_spec, grouped_spec],
        out_shape=[dense, dense, grouped, grouped],
        compiler_params=_params(("parallel", "arbitrary")),
        name="peer_score",
    )(xt, wq_t, keys)


def _gelu(x):
    return 0.5 * x * (1.0 + lax.erf(x * (2.0 ** -0.5)))


def _build_gate_weights(r2_ref, e2_ref, cnt_ref, c_ref, w_ref, rows_ref, sub, n_i):
    tm = w_ref.shape[1]
    groups_per_tile = SUBLANES // n_i
    chunk_rows = SUBLANES * MIX_ACC_VREGS // n_i
    for h in range(PEER_HEADS):
        for ii in range(n_i):
            cnt_row = cnt_ref[0, h, ii:ii + 1, :]
            c_row = c_ref[0, h, ii:ii + 1, :]
            for alt in range(1, groups_per_tile):
                r = alt * n_i + ii
                cnt_row = jnp.where(sub == alt, cnt_ref[0, h, r:r + 1, :], cnt_row)
                c_row = jnp.where(sub == alt, c_ref[0, h, r:r + 1, :], c_row)
            rows_ref[0, h * n_i + ii] = jnp.broadcast_to(cnt_row, (SUBLANES, tm)).astype(BF16)
            rows_ref[1, h * n_i + ii] = jnp.broadcast_to(c_row, (SUBLANES, tm)).astype(BF16)
    for tl in range(tm // BF16_COLS):
        cols = slice(tl * BF16_COLS, (tl + 1) * BF16_COLS)
        for jc in range(N_KEYS // chunk_rows):
            rows = slice(jc * chunk_rows, (jc + 1) * chunk_rows)
            acc = [None] * n_i
            packed = (chunk_rows // SUBLANES, SUBLANES, BF16_COLS)
            for h in range(PEER_HEADS):
                hrows = slice(h * N_KEYS + rows.start, h * N_KEYS + rows.stop)
                r2c = r2_ref[hrows, cols].reshape(packed)
                e2c = e2_ref[hrows, cols].reshape(packed)
                for ii in range(n_i):
                    cnt_b = rows_ref[0, h * n_i + ii, :, cols][None]
                    c_b = rows_ref[1, h * n_i + ii, :, cols][None]
                    term = jnp.where(r2c < cnt_b, e2c * c_b, jnp.zeros((), BF16))
                    acc[ii] = term if acc[ii] is None else acc[ii] + term
            for ii in range(n_i):
                r0 = ii * N_KEYS + jc * chunk_rows
                w_ref[r0:r0 + chunk_rows, cols] = acc[ii].reshape(chunk_rows, BF16_COLS)


def _peer_mix_kernel(xt_ref, wd_ref, wu_ref, r2_ref, e2_ref, cnt_ref, c_ref, o_ref, w_ref,
                     rows_ref, *, n_i):
    e = pl.program_id(1)

    @pl.when(e == 0)
    def _():
        o_ref[...] = jnp.zeros_like(o_ref)

    _build_gate_weights(r2_ref, e2_ref, cnt_ref, c_ref, w_ref, rows_ref,
                        e % (SUBLANES // n_i), n_i)
    a = jnp.dot(wd_ref[...], xt_ref[...], preferred_element_type=F32)
    gated = _gelu(a).astype(BF16) * w_ref[...]
    o_ref[...] += jnp.dot(wu_ref[...], gated, preferred_element_type=F32)


def _peer_mix(xt, w_down, w_up_t, rank2, e2, cnt, c, *, tm, eb):
    d, t = xt.shape
    ne = w_down.shape[0]
    n_i = eb // N_KEYS
    groups_per_tile = SUBLANES // n_i
    once = pl.Buffered(1)
    dense_spec = pl.BlockSpec((PEER_HEADS * N_KEYS, tm), lambda i, e: (0, i),
                              pipeline_mode=once)
    grouped_spec = pl.BlockSpec((1, PEER_HEADS, SUBLANES, tm),
                                lambda i, e: (e // groups_per_tile, 0, 0, i))
    return pl.pallas_call(
        functools.partial(_peer_mix_kernel, n_i=n_i),
        grid=(t // tm, ne // eb),
        in_specs=[pl.BlockSpec((d, tm), lambda i, e: (0, i), pipeline_mode=once),
                  pl.BlockSpec((eb, d), lambda i, e: (e, 0)),
                  pl.BlockSpec((d, eb), lambda i, e: (0, e)),
                  dense_spec, dense_spec, grouped_spec, grouped_spec],
        out_specs=pl.BlockSpec((d, tm), lambda i, e: (0, i), pipeline_mode=once),
        out_shape=jax.ShapeDtypeStruct((d, t), F32),
        scratch_shapes=[pltpu.VMEM((eb, tm), BF16),
                        pltpu.VMEM((2, PEER_HEADS * n_i, SUBLANES, tm), BF16)],
        compiler_params=_params(("parallel", "arbitrary")),
        name="peer_mix",
    )(xt, w_down, w_up_t, rank2, e2, cnt, c)


def _add_t_kernel(x_ref, pt_ref, o_ref):
    o_ref[...] = x_ref[...] + pt_ref[...].T


def _add_transposed(x, pt, *, tm):
    t, d = x.shape
    return pl.pallas_call(
        _add_t_kernel,
        grid=(t // tm,),
        in_specs=[pl.BlockSpec((tm, d), lambda i: (i, 0)),
                  pl.BlockSpec((d, tm), lambda i: (0, i))],
        out_specs=pl.BlockSpec((tm, d), lambda i: (i, 0)),
        out_shape=jax.ShapeDtypeStruct((t, d), F32),
        compiler_params=_params(("parallel",)),
        name="add_transposed",
    )(x, pt)


def _tiles(t, d):
    def pick(n, cap):
        b = cap
        while n % b:
            b //= 2
        return b
    return dict(
        norm_tm=pick(t, 512),
        mm_tm=pick(t, 1024),
        mm_tn=pick(d // 4, 1024),
        conv_tm=pick(t, 512),
        peer_tm=pick(t, 1024),
        score_tm=pick(t, 1024),
        peer_eb=512,
        cast_tr=256,
    )


def _layer(x2, seq, norm1_g, w_in, q_norm_g, k_norm_g, sink_logits, conv_w, w_o_attn,
           w_o_conv, w_out, norm2_g, w_q_peer, sub_keys, w_down, w_up):
    t, d = x2.shape
    q_w = w_o_attn.shape[0]
    kv_w = q_w // GQA_GROUP
    conv_width = w_o_conv.shape[0]
    off_h = q_w + 2 * kv_w
    off_ga = off_h + 3 * conv_width
    off_gc = off_ga + d
    tl = _tiles(t, d)

    xn = _rmsnorm(x2, norm1_g, tm=tl["norm_tm"])
    proj = _matmul(xn, w_in.astype(BF16), tm=tl["mm_tm"], tn=tl["mm_tn"],
                   out_dtype=BF16, name="in_proj")
    attn = _attention(proj, sink_logits.astype(F32), q_norm_g, k_norm_g,
                      seq=seq, q_w=q_w, kv_w=kv_w)
    conv = _short_conv(proj, conv_w, seq=seq, off_h=off_h, conv_width=conv_width,
                       tm=tl["conv_tm"])
    merged = _merge(attn, conv, w_o_attn.astype(BF16), w_o_conv.astype(BF16), proj,
                    off_ga=off_ga, off_gc=off_gc, tm=tl["mm_tm"], tn=tl["mm_tn"])
    x1 = _matmul_residual(x2, merged, w_out.astype(BF16), tm=tl["mm_tm"], tn=tl["mm_tn"])

    xn2_t = _rmsnorm(x1, norm2_g, tm=tl["norm_tm"], transpose=True)
    wq_t = _cast_transpose(w_q_peer, tr=tl["cast_tr"])
    rank2, e2, cnt, c = _peer_score(xn2_t, wq_t, sub_keys.astype(BF16),
                                    tm=tl["score_tm"])
    w_up_t = _cast_transpose(w_up, tr=tl["cast_tr"])
    peer_t = _peer_mix(xn2_t, w_down.astype(BF16), w_up_t, rank2, e2, cnt, c,
                       tm=tl["peer_tm"], eb=tl["peer_eb"])
    return _add_transposed(x1, peer_t, tm=tl["norm_tm"])


def kernel(x, norm1_g, w_in, q_norm_g, k_norm_g, sink_logits, conv_w, w_o_attn, w_o_conv,
           w_out, norm2_g, w_q_peer, sub_keys, w_down, w_up):
    b, s, d = x.shape
    x2 = x.reshape(b * s, d)
    for i in range(norm1_g.shape[0]):
        x2 = _layer(x2, s, norm1_g[i], w_in[i], q_norm_g[i], k_norm_g[i], sink_logits[i],
                    conv_w[i], w_o_attn[i], w_o_conv[i], w_out[i], norm2_g[i],
                    w_q_peer[i], sub_keys[i], w_down[i], w_up[i])
    return x2.reshape(b, s, d)
```
